```python
import math
import jax, jax.numpy as jnp
from jax import lax
import numpy as np

D_MODEL = 2048
BATCH = 1
SEQ = 16384
DEPTH = 1
DEC_BATCH = 32
DEC_SEQ = 64
PAST_LEN = 4096

CHUNK = 64
Q_BLOCK = 128
A_HEADS = 8
A_KV_HEADS = 2
A_HEAD_DIM = 128
A_GROUP = A_HEADS // A_KV_HEADS
A_WIDTH = A_HEADS * A_HEAD_DIM
IDX_HEADS = 8
IDX_DIM = 64
TOPK_MAX = 256
B_HEADS = 8
B_KEY_DIM = 128
B_VAL_DIM = 128
B_QK_WIDTH = B_HEADS * B_KEY_DIM
B_WIDTH = B_HEADS * B_VAL_DIM
CONV_WIDTH = 4
CONV_CH = 2 * B_QK_WIDTH + B_WIDTH
D_FF = 5632
N_MOD = 9
N_NORMS = 6
FFN_WEIGHT = 0.5
EPS = 1e-6
IN_SPLITS = (A_WIDTH, A_KV_HEADS * A_HEAD_DIM, A_KV_HEADS * A_HEAD_DIM, IDX_HEADS * IDX_DIM, IDX_DIM, IDX_HEADS,
             B_QK_WIDTH, B_QK_WIDTH, B_WIDTH, B_WIDTH, B_HEADS, B_HEADS, D_MODEL, D_MODEL)
IN_COLS = 10328

kernel_name = "hybrid_dsa_gdn_macaron_stream_step"


def _split_points():
    pts, acc = [], 0
    for s in IN_SPLITS[:-1]:
        acc += s
        pts.append(acc)
    return pts


def rms_norm(x, g):
    xf = x.astype(jnp.float32)
    y = xf * lax.rsqrt(jnp.mean(xf * xf, axis=-1, keepdims=True) + EPS)
    return (y * g.astype(jnp.float32)).astype(x.dtype)


def layer_norm(x, g, b):
    xf = x.astype(jnp.float32)
    mu = jnp.mean(xf, axis=-1, keepdims=True)
    xc = xf - mu
    y = xc * lax.rsqrt(jnp.mean(xc * xc, axis=-1, keepdims=True) + EPS)
    return (y * g.astype(jnp.float32) + b.astype(jnp.float32)).astype(x.dtype)


def l2_normalize(x):
    xf = x.astype(jnp.float32)
    return xf * lax.rsqrt(jnp.sum(xf * xf, axis=-1, keepdims=True) + EPS)


def swiglu(h, w_gate, w_up, w_down):
    return (jax.nn.silu(h @ w_gate) * (h @ w_up)) @ w_down


def causal_conv(xc, buf, w):
    xp = jnp.concatenate([buf.astype(xc.dtype), xc], axis=1)
    y = lax.conv_general_dilated(xp, w[:, None, :].astype(xc.dtype), window_strides=(1,), padding='VALID',
                                 dimension_numbers=('NWC', 'WIO', 'NWC'), feature_group_count=xc.shape[-1])
    return jax.nn.silu(y), xp[:, xp.shape[1] - (CONV_WIDTH - 1):]


def dsa_attend(q, qi, wi, q_pos, k, v, ki, k_pos, topk):
    f32 = jnp.float32
    B, Tq = q.shape[0], q.shape[1]
    s = jnp.einsum('bthd,bsd->bths', qi.astype(f32), ki.astype(f32))
    w = wi.astype(f32) * (IDX_HEADS ** -0.5 * IDX_DIM ** -0.5)
    score = jnp.einsum('bth,bths->bts', w, jax.nn.relu(s))
    allowed = (k_pos[None, :] // CHUNK) <= (q_pos[:, None] // CHUNK)
    score = jnp.where(allowed[None], score, -jnp.inf)
    top_val, idx = lax.top_k(score, topk)
    valid = jnp.isfinite(top_val)
    gather = jax.vmap(lambda a, i: a[i])
    k_sel = gather(k, idx)
    v_sel = gather(v, idx)
    qg = q.reshape(B, Tq, A_KV_HEADS, A_GROUP, A_HEAD_DIM)
    logits = jnp.einsum('btngd,btjnd->btngj', qg, k_sel).astype(f32) * (A_HEAD_DIM ** -0.5)
    logits = jnp.where(valid[:, :, None, None, :], logits, -jnp.inf)
    p = jax.nn.softmax(logits, axis=-1).astype(v.dtype)
    o = jnp.einsum('btngj,btjnd->btngd', p, v_sel)
    return o.reshape(B, Tq, A_WIDTH)


def gated_delta_rule(q, k, v, g, beta, s0, chunk):
    f32 = jnp.float32
    B, T, H, K = q.shape
    V = v.shape[-1]
    N, C = T // chunk, chunk

    def blocks(a):
        a = a.astype(f32).reshape((B, N, C, H) + a.shape[3:])
        return jnp.moveaxis(a, (1, 3), (0, 2))

    q, k, v, g, beta = blocks(q), blocks(k), blocks(v), blocks(g), blocks(beta)
    gc = jnp.cumsum(g, axis=-1)
    tril = jnp.tril(jnp.ones((C, C), bool))
    strict = jnp.tril(jnp.ones((C, C), bool), -1)
    diff = gc[..., :, None] - gc[..., None, :]
    decay = jnp.where(tril, jnp.exp(jnp.where(tril, diff, 0.0)), 0.0)
    kb = k * beta[..., None]
    lower = jnp.where(strict, jnp.einsum('nbhik,nbhjk->nbhij', kb, k) * decay, 0.0)
    a_mat = lower + jnp.eye(C, dtype=f32)
    rhs = jnp.concatenate([v * beta[..., None], kb * jnp.exp(gc)[..., None]], axis=-1)
    sol = lax.linalg.triangular_solve(a_mat, rhs, left_side=True, lower=True, unit_diagonal=True)
    u, w = sol[..., :V], sol[..., V:]
    qk = jnp.einsum('nbhik,nbhjk->nbhij', q, k) * decay

    def step(S, xs):
        qc, kc, uc, wc, gcc, qkc = xs
        v_new = uc - jnp.einsum('bhck,bhkv->bhcv', wc, S)
        o = (jnp.einsum('bhck,bhkv->bhcv', qc * jnp.exp(gcc)[..., None], S)
             + jnp.einsum('bhij,bhjv->bhiv', qkc, v_new))
        g_last = gcc[..., -1:]
        S = (S * jnp.exp(g_last)[..., None]
             + jnp.einsum('bhck,bhcv->bhkv', kc * jnp.exp(g_last - gcc)[..., None], v_new))
        return S, o

    S, o = lax.scan(step, s0.astype(f32), (q, k, u, w, gc, qk))
    o = jnp.moveaxis(o, (0, 2), (1, 3)).reshape(B, T, H, V)
    return o, S


def setup_inputs(seed: int = 0) -> dict:
    key = jax.random.key(seed)
    ks = iter(jax.random.split(key, 48))
    f32 = jnp.float32

    def nrm(shape, scale):
        return jax.random.normal(next(ks), shape, f32) * scale

    def gain(shape):
        return 1.0 + nrm(shape, 0.02)

    L = DEPTH
    dt = jnp.exp(jax.random.uniform(next(ks), (L, B_HEADS), f32, math.log(1e-3), math.log(1e-1)))
    return {
        "x_prompt": nrm((BATCH, SEQ, D_MODEL), 1.0),
        "x_sample": nrm((DEC_BATCH, DEC_SEQ, D_MODEL), 1.0),
        "cache_k": nrm((L, DEC_BATCH, PAST_LEN, A_KV_HEADS, A_HEAD_DIM), 1.0),
        "cache_v": nrm((L, DEC_BATCH, PAST_LEN, A_KV_HEADS, A_HEAD_DIM), 1.0),
        "cache_kidx": nrm((L, DEC_BATCH, PAST_LEN, IDX_DIM), 1.0),
        "state_conv": nrm((L, DEC_BATCH, CONV_WIDTH - 1, CONV_CH), 1.0),
        "state_gdn": nrm((L, DEC_BATCH, B_HEADS, B_KEY_DIM, B_VAL_DIM), 0.1),
        "c_prompt": nrm((BATCH, D_MODEL), 1.0),
        "c_sample": nrm((DEC_BATCH, D_MODEL), 1.0),
        "w_ada": nrm((L, D_MODEL, N_MOD * D_MODEL), 0.5 * D_MODEL ** -0.5),
        "b_ada": nrm((L, N_MOD * D_MODEL), 0.02),
        "norm_g": gain((L, N_NORMS, D_MODEL)),
        "w1_gate": nrm((L, D_MODEL, D_FF), D_MODEL ** -0.5),
        "w1_up": nrm((L, D_MODEL, D_FF), D_MODEL ** -0.5),
        "w1_down": nrm((L, D_FF, D_MODEL), D_FF ** -0.5),
        "w_in": nrm((L, D_MODEL, IN_COLS), D_MODEL ** -0.5),
        "idx_ln_g": gain((L, IDX_DIM)),
        "idx_ln_b": nrm((L, IDX_DIM), 0.02),
        "conv_w": nrm((L, CONV_WIDTH, CONV_CH), CONV_WIDTH ** -0.5),
        "a_log": jnp.log(jax.random.uniform(next(ks), (L, B_HEADS), f32, 1.0, 16.0)),
        "dt_bias": dt + jnp.log(-jnp.expm1(-dt)),
        "gdn_norm_g": gain((L, B_VAL_DIM)),
        "w_branch_a": nrm((L, A_WIDTH, D_MODEL), A_WIDTH ** -0.5),
        "w_branch_b": nrm((L, B_WIDTH, D_MODEL), B_WIDTH ** -0.5),
        "w_out": nrm((L, D_MODEL, D_MODEL), D_MODEL ** -0.5),
        "w2_gate": nrm((L, D_MODEL, D_FF), D_MODEL ** -0.5),
        "w2_up": nrm((L, D_MODEL, D_FF), D_MODEL ** -0.5),
        "w2_down": nrm((L, D_FF, D_MODEL), D_FF ** -0.5),
    }


def reference(x_prompt, x_sample, cache_k, cache_v, cache_kidx, state_conv, state_gdn, c_prompt, c_sample,
              w_ada, b_ada, norm_g, w1_gate, w1_up, w1_down, w_in, idx_ln_g, idx_ln_b, conv_w, a_log, dt_bias,
              gdn_norm_g, w_branch_a, w_branch_b, w_out, w2_gate, w2_up, w2_down):
    f32 = jnp.float32
    split_pts = _split_points()

    def token_mixer(h, l, cache):
        B, T, _ = h.shape
        (qa, ka, va, qi, ki, wi, qb, kb, vb, zb, ab, bb, ga, gb) = jnp.split(h @ w_in[l], split_pts, axis=-1)
        qa = qa.reshape(B, T, A_HEADS, A_HEAD_DIM)
        ka = ka.reshape(B, T, A_KV_HEADS, A_HEAD_DIM)
        va = va.reshape(B, T, A_KV_HEADS, A_HEAD_DIM)
        qi = qi.reshape(B, T, IDX_HEADS, IDX_DIM)
        ki = layer_norm(ki, idx_ln_g[l], idx_ln_b[l])
        if cache is None:
            past = 0
            k_all, v_all, ki_all = ka, va, ki
            conv_buf = jnp.zeros((B, CONV_WIDTH - 1, CONV_CH), h.dtype)
            s0 = jnp.zeros((B, B_HEADS, B_KEY_DIM, B_VAL_DIM), f32)
        else:
            ck, cv, cki, conv_buf, s0 = cache
            past = ck.shape[1]
            k_all = jnp.concatenate([ck, ka], axis=1)
            v_all = jnp.concatenate([cv, va], axis=1)
            ki_all = jnp.concatenate([cki, ki], axis=1)
        n_keys = past + T
        topk = min(TOPK_MAX, n_keys // 4)
        q_pos = past + jnp.arange(T)
        k_pos = jnp.arange(n_keys)
        if T % Q_BLOCK == 0:
            nb = T // Q_BLOCK

            def blk(a):
                return jnp.moveaxis(a.reshape((B, nb, Q_BLOCK) + a.shape[2:]), 1, 0)

            oa = lax.map(lambda z: dsa_attend(z[0], z[1], z[2], z[3], k_all, v_all, ki_all, k_pos, topk),
                         (blk(qa), blk(qi), blk(wi), q_pos.reshape(nb, Q_BLOCK)))
            oa = jnp.moveaxis(oa, 0, 1).reshape(B, T, A_WIDTH)
        else:
            oa = dsa_attend(qa, qi, wi, q_pos, k_all, v_all, ki_all, k_pos, topk)
        xconv, new_buf = causal_conv(jnp.concatenate([qb, kb, vb], axis=-1), conv_buf, conv_w[l])
        qb, kb, vb = jnp.split(xconv, [B_QK_WIDTH, 2 * B_QK_WIDTH], axis=-1)
        qb = l2_normalize(qb.reshape(B, T, B_HEADS, B_KEY_DIM)) * (B_KEY_DIM ** -0.5)
        kb = l2_normalize(kb.reshape(B, T, B_HEADS, B_KEY_DIM))
        vb = vb.reshape(B, T, B_HEADS, B_VAL_DIM)
        beta = jax.nn.sigmoid(bb.astype(f32))
        g = -jnp.exp(a_log[l].astype(f32)) * jax.nn.softplus(ab.astype(f32) + dt_bias[l].astype(f32))
        chunk = CHUNK if T % CHUNK == 0 else T
        ob, s_new = gated_delta_rule(qb, kb, vb, g, beta, s0, chunk)
        ob = rms_norm(ob, gdn_norm_g[l]) * jax.nn.silu(zb.astype(f32).reshape(B, T, B_HEADS, B_VAL_DIM))
        ob = ob.astype(h.dtype).reshape(B, T, B_WIDTH)
        merged = jax.nn.sigmoid(ga) * (oa @ w_branch_a[l]) + jax.nn.sigmoid(gb) * (ob @ w_branch_b[l])
        return merged @ w_out[l], (ka, va, ki, new_buf, s_new)

    def layer(x, c, l, cache):
        mod = (c @ w_ada[l] + b_ada[l]).reshape(c.shape[0], N_MOD, D_MODEL)
        sh1, sc1, gt1, sh2, sc2, gt2, sh3, sc3, gt3 = [mod[:, i, None, :] for i in range(N_MOD)]
        h = rms_norm(x, norm_g[l, 0]) * (1.0 + sc1) + sh1
        x = x + FFN_WEIGHT * gt1 * rms_norm(swiglu(h, w1_gate[l], w1_up[l], w1_down[l]), norm_g[l, 1])
        h = rms_norm(x, norm_g[l, 2]) * (1.0 + sc2) + sh2
        y, st = token_mixer(h, l, cache)
        x = x + gt2 * rms_norm(y, norm_g[l, 3])
        h = rms_norm(x, norm_g[l, 4]) * (1.0 + sc3) + sh3
        x = x + FFN_WEIGHT * gt3 * rms_norm(swiglu(h, w2_gate[l], w2_up[l], w2_down[l]), norm_g[l, 5])
        return x, st

    def run(x, c, caches):
        new = []
        for l in range(DEPTH):
            cache = None if caches is None else tuple(a[l] for a in caches)
            x, st = layer(x, c, l, cache)
            new.append(st)
        return x, [jnp.stack([s[i] for s in new], axis=0) for i in range(5)]

    y_prompt, (k_p, v_p, kidx_p, conv_p, gdn_p) = run(x_prompt, c_prompt, None)
    y_sample, (k_s, v_s, kidx_s, conv_s, gdn_s) = run(
        x_sample, c_sample, (cache_k, cache_v, cache_kidx, state_conv, state_gdn))
    return (y_prompt, y_sample, k_p, v_p, kidx_p, conv_p, gdn_p, k_s, v_s, kidx_s, conv_s, gdn_s)
```

```python
import functools

import jax
import jax.numpy as jnp
from jax import lax
from jax.experimental import pallas as pl
from jax.experimental.pallas import tpu as pltpu

f32 = jnp.float32
bf16 = jnp.bfloat16
i32 = jnp.int32

CHUNK = 64
A_HEADS, A_KV_HEADS, A_HEAD_DIM = 8, 2, 128
A_GROUP = A_HEADS // A_KV_HEADS
A_WIDTH = A_HEADS * A_HEAD_DIM
KV_WIDTH = A_KV_HEADS * A_HEAD_DIM
IDX_HEADS, IDX_DIM = 8, 64
TOPK_MAX = 256
B_HEADS, B_KEY_DIM, B_VAL_DIM = 8, 128, 128
B_QK_WIDTH = B_HEADS * B_KEY_DIM
B_WIDTH = B_HEADS * B_VAL_DIM
CONV_WIDTH = 4
CONV_CH = 2 * B_QK_WIDTH + B_WIDTH
N_MOD = 9
FFN_WEIGHT = 0.5
EPS = 1e-6

COL_QKVB, COL_QA, COL_GA, COL_GB, COL_ZB, COL_KV, COL_QI, COL_SM = 0, 3072, 4096, 6144, 8192, 9216, 9728, 10240
SM_WIDTH = 128
SM_WI, SM_AB, SM_BB = 64, 72, 80
PROJ_COLS = COL_SM + SM_WIDTH

INT_MIN = -(2 ** 31)
INT_MAX = 2 ** 31 - 1
NEG = -1e30
VMEM_LIMIT = 56 * 1024 * 1024


def _pick(n, prefs):
    for p in prefs:
        if n % p == 0:
            return p
    raise ValueError(f"no tile of {prefs} divides {n}")


def _params(*sem):
    return pltpu.CompilerParams(dimension_semantics=sem, vmem_limit_bytes=VMEM_LIMIT)


def _rms(y, g):
    return y * lax.rsqrt(jnp.mean(y * y, axis=-1, keepdims=True) + EPS) * g


def _ada_kernel(c_ref, w_ref, b_ref, o_ref):
    o_ref[...] = jnp.dot(c_ref[...].astype(bf16), w_ref[...].astype(bf16),
                         preferred_element_type=f32) + b_ref[...]


def _ada(c, w, b):
    m, d = c.shape
    n = w.shape[1]
    tn = _pick(n, (1024, 512, 256, 128))
    return pl.pallas_call(
        _ada_kernel,
        grid=(n // tn,),
        in_specs=[pl.BlockSpec((m, d), lambda j: (0, 0)),
                  pl.BlockSpec((d, tn), lambda j: (0, j)),
                  pl.BlockSpec((1, tn), lambda j: (0, j))],
        out_specs=pl.BlockSpec((m, tn), lambda j: (0, j)),
        out_shape=jax.ShapeDtypeStruct((m, n), f32),
        compiler_params=_params("arbitrary"),
        name="ada",
    )(c, w, b)


def _norm_mod_kernel(x_ref, mod_ref, g_ref, o_ref, *, sh, sc, nb):
    g = g_ref[...]
    for j in range(nb):
        rows = slice(j * CHUNK, (j + 1) * CHUNK)
        y = _rms(x_ref[rows, :], g)
        o_ref[rows, :] = (y * (1.0 + mod_ref[j, sc:sc + 1, :]) + mod_ref[j, sh:sh + 1, :]).astype(bf16)


def _norm_mod(x, modg, g, sh, sc):
    m, d = x.shape
    tm = _pick(m, (512, 256, 128, 64))
    nb = tm // CHUNK
    return pl.pallas_call(
        functools.partial(_norm_mod_kernel, sh=sh, sc=sc, nb=nb),
        grid=(m // tm,),
        in_specs=[pl.BlockSpec((tm, d), lambda i: (i, 0)),
                  pl.BlockSpec((nb, N_MOD, d), lambda i: (i, 0, 0)),
                  pl.BlockSpec((1, d), lambda i: (0, 0))],
        out_specs=pl.BlockSpec((tm, d), lambda i: (i, 0)),
        out_shape=jax.ShapeDtypeStruct((m, d), bf16),
        compiler_params=_params("arbitrary"),
        name="norm_mod",
    )(x, modg, g)


def _gateup_kernel(h_ref, wg_ref, wu_ref, o_ref):
    h = h_ref[...]
    a = jnp.dot(h, wg_ref[...], preferred_element_type=f32)
    u = jnp.dot(h, wu_ref[...], preferred_element_type=f32)
    o_ref[...] = (a * jax.nn.sigmoid(a) * u).astype(bf16)


def _gateup(h, wg, wu):
    m, d = h.shape
    n = wg.shape[1]
    tm = _pick(m, (1024, 512, 256, 128, 64))
    tn = _pick(n, (512, 256, 128))
    return pl.pallas_call(
        _gateup_kernel,
        grid=(m // tm, n // tn),
        in_specs=[pl.BlockSpec((tm, d), lambda i, j: (i, 0)),
                  pl.BlockSpec((d, tn), lambda i, j: (0, j)),
                  pl.BlockSpec((d, tn), lambda i, j: (0, j))],
        out_specs=pl.BlockSpec((tm, tn), lambda i, j: (i, j)),
        out_shape=jax.ShapeDtypeStruct((m, n), bf16),
        compiler_params=_params("arbitrary", "arbitrary"),
        name="gateup",
    )(h, wg, wu)


def _mm_norm_res_kernel(a_ref, w_ref, x_ref, mod_ref, g_ref, o_ref, acc_ref, *, gt, coeff, nb, nk):
    k = pl.program_id(1)

    @pl.when(k == 0)
    def _():
        acc_ref[...] = jnp.zeros_like(acc_ref)

    acc_ref[...] += jnp.dot(a_ref[...], w_ref[...], preferred_element_type=f32)

    @pl.when(k == nk - 1)
    def _():
        g = g_ref[...]
        for j in range(nb):
            rows = slice(j * CHUNK, (j + 1) * CHUNK)
            gate = mod_ref[j, gt:gt + 1, :]
            if coeff is not None:
                gate = coeff * gate
            o_ref[rows, :] = x_ref[rows, :] + gate * _rms(acc_ref[rows, :], g)


def _mm_norm_res(a, w, x, modg, g, gt, coeff):
    m, kd = a.shape
    d = w.shape[1]
    tm = _pick(m, (512, 256, 128, 64))
    tk = _pick(kd, (512, 256, 128))
    nb, nk = tm // CHUNK, kd // tk
    return pl.pallas_call(
        functools.partial(_mm_norm_res_kernel, gt=gt, coeff=coeff, nb=nb, nk=nk),
        grid=(m // tm, nk),
        in_specs=[pl.BlockSpec((tm, tk), lambda i, k: (i, k)),
                  pl.BlockSpec((tk, d), lambda i, k: (k, 0)),
                  pl.BlockSpec((tm, d), lambda i, k: (i, 0)),
                  pl.BlockSpec((nb, N_MOD, d), lambda i, k: (i, 0, 0)),
                  pl.BlockSpec((1, d), lambda i, k: (0, 0))],
        out_specs=pl.BlockSpec((tm, d), lambda i, k: (i, 0)),
        out_shape=jax.ShapeDtypeStruct((m, d), f32),
        scratch_shapes=[pltpu.VMEM((tm, d), f32)],
        compiler_params=_params("arbitrary", "arbitrary"),
        name="mm_norm_res",
    )(a, w, x, modg, g)


def _mm_kernel(a_ref, w_ref, o_ref):
    o_ref[...] = jnp.dot(a_ref[...], w_ref[...], preferred_element_type=f32)


def _mm(a, w):
    m, kd = a.shape
    n = w.shape[1]
    tm = _pick(m, (1024, 512, 256, 128, 64))
    tn = _pick(n, (1152, 1024, 512, 384, 256, 128))
    return pl.pallas_call(
        _mm_kernel,
        grid=(m // tm, n // tn),
        in_specs=[pl.BlockSpec((tm, kd), lambda i, j: (i, 0)),
                  pl.BlockSpec((kd, tn), lambda i, j: (0, j))],
        out_specs=pl.BlockSpec((tm, tn), lambda i, j: (i, j)),
        out_shape=jax.ShapeDtypeStruct((m, n), f32),
        compiler_params=_params("arbitrary", "arbitrary"),
        name="in_proj",
    )(a, w)


def _kidx_kernel(sm_ref, g_ref, b_ref, o_ref):
    x = sm_ref[:, 0:IDX_DIM]
    xc = x - jnp.mean(x, axis=-1, keepdims=True)
    y = xc * lax.rsqrt(jnp.mean(xc * xc, axis=-1, keepdims=True) + EPS)
    o_ref[...] = y * g_ref[...] + b_ref[...]


def _kidx(proj, g, b):
    m = proj.shape[0]
    tm = _pick(m, (1024, 512, 256, 128, 64))
    return pl.pallas_call(
        _kidx_kernel,
        grid=(m // tm,),
        in_specs=[pl.BlockSpec((tm, SM_WIDTH), lambda i: (i, COL_SM // SM_WIDTH)),
                  pl.BlockSpec((1, IDX_DIM), lambda i: (0, 0)),
                  pl.BlockSpec((1, IDX_DIM), lambda i: (0, 0))],
        out_specs=pl.BlockSpec((tm, IDX_DIM), lambda i: (i, 0)),
        out_shape=jax.ShapeDtypeStruct((m, IDX_DIM), f32),
        compiler_params=_params("arbitrary"),
        name="kidx_ln",
    )(proj, g, b)


def _dsa_kernel(qi_ref, sm_ref, qa_ref, kit_ref, kt_ref, v_ref, o_ref,
                keys_ref, q_scr, m_scr, l_scr, acc_scr, *, tq, tk, past, n_keys, topk, nbits):
    chunk_shift = CHUNK.bit_length() - 1
    qpos0 = past + pl.program_id(1) * tq
    limit = jnp.minimum((((qpos0 + tq - 1) >> chunk_shift) + 1) * CHUNK, n_keys)
    nkb = (limit + tk - 1) // tk
    row = lax.broadcasted_iota(i32, (tq, tk), 0)
    lane = lax.broadcasted_iota(i32, (tq, tk), 1)
    qchunk = (qpos0 + row) >> chunk_shift

    qi = qi_ref[...].astype(bf16)
    qis = [qi[:, h * IDX_DIM:(h + 1) * IDX_DIM] for h in range(IDX_HEADS)]
    w = sm_ref[:, SM_WI:SM_WI + IDX_HEADS] * (IDX_HEADS ** -0.5 * IDX_DIM ** -0.5)
    ws = [w[:, h:h + 1] for h in range(IDX_HEADS)]

    def score_body(kb, carry):
        c0 = pl.multiple_of(kb * tk, tk)
        kit = kit_ref[:, pl.ds(c0, tk)]
        sc = jnp.zeros((tq, tk), f32)
        for h in range(IDX_HEADS):
            s = jnp.dot(qis[h], kit, preferred_element_type=f32)
            sc = sc + ws[h] * jnp.maximum(s, 0.0)
        bits = lax.bitcast_convert_type(sc, i32)
        key = bits ^ ((bits >> 31) & INT_MAX)
        key = jnp.where(sc == 0.0, 0, key)
        kpos = c0 + lane
        allowed = ((kpos >> chunk_shift) <= qchunk) & (kpos < n_keys)
        keys_ref[:, pl.ds(c0, tk)] = jnp.where(allowed, key, INT_MIN)
        return carry

    lax.fori_loop(0, nkb, score_body, 0)

    def count(pred):
        def body(kb, acc):
            c0 = pl.multiple_of(kb * tk, tk)
            x = jnp.where(pred(keys_ref[:, pl.ds(c0, tk)], c0), 1.0, 0.0)
            for c in range(tk // 128):
                acc = acc + x[:, c * 128:(c + 1) * 128]
            return acc
        acc = lax.fori_loop(0, nkb, body, jnp.zeros((tq, 128), f32))
        return jnp.sum(acc, axis=1, keepdims=True)

    kf = float(topk)

    def bit_body(it, carry):
        t, ct = carry
        cand = t + lax.shift_left(jnp.int32(1), 31 - it)
        cnt = count(lambda blk, c0: blk >= cand)
        take = cnt >= kf
        return jnp.where(take, cand, t), jnp.where(take, cnt, ct)

    t0 = jnp.full((tq, 1), INT_MIN, i32)
    ct0 = jnp.zeros((tq, 1), f32) + (nkb * tk).astype(f32)
    t, ct = lax.fori_loop(0, 32, bit_body, (t0, ct0))

    def tie_cut():
        r = kf - count(lambda blk, c0: blk > t)

        def j_body(it, j):
            cand = j + lax.shift_left(jnp.int32(1), nbits - 1 - it)
            cnt = count(lambda blk, c0: (blk == t) & ((c0 + lane) < cand))
            return jnp.where(cnt < r, cand, j)

        return lax.fori_loop(0, nbits, j_body, jnp.zeros((tq, 1), i32))

    jcut = lax.cond(jnp.max(ct) > kf, tie_cut, lambda: jnp.full((tq, 1), INT_MAX, i32))
    tsel = jnp.maximum(t, INT_MIN + 1)

    qa = qa_ref[...]
    for n in range(A_KV_HEADS):
        q_scr[n] = jnp.concatenate(
            [qa[:, (n * A_GROUP + g) * A_HEAD_DIM:(n * A_GROUP + g + 1) * A_HEAD_DIM] for g in range(A_GROUP)],
            axis=0).astype(bf16)
    m_scr[...] = jnp.full(m_scr.shape, NEG, f32)
    l_scr[...] = jnp.zeros(l_scr.shape, f32)
    acc_scr[...] = jnp.zeros(acc_scr.shape, f32)
    scale = A_HEAD_DIM ** -0.5

    def att_body(kb, carry):
        c0 = pl.multiple_of(kb * tk, tk)
        key = keys_ref[:, pl.ds(c0, tk)]
        sel = (key > tsel) | ((key == tsel) & ((c0 + lane) <= jcut))
        bias = jnp.where(sel, 0.0, 2.0 * NEG)
        bias = jnp.concatenate([bias] * A_GROUP, axis=0)
        for n in range(A_KV_HEADS):
            kt = kt_ref[n * A_HEAD_DIM:(n + 1) * A_HEAD_DIM, pl.ds(c0, tk)]
            vv = v_ref[pl.ds(c0, tk), n * A_HEAD_DIM:(n + 1) * A_HEAD_DIM]
            s = jnp.dot(q_scr[n], kt, preferred_element_type=f32) * scale + bias
            m_old = m_scr[n]
            m_new = jnp.maximum(m_old, jnp.max(s, axis=1, keepdims=True))
            p = jnp.exp(s - m_new)
            alpha = jnp.exp(m_old - m_new)
            l_scr[n] = alpha * l_scr[n] + jnp.sum(p, axis=1, keepdims=True)
            acc_scr[n] = alpha * acc_scr[n] + jnp.dot(p.astype(bf16), vv, preferred_element_type=f32)
            m_scr[n] = m_new
        return carry

    lax.fori_loop(0, nkb, att_body, 0)

    for n in range(A_KV_HEADS):
        o = acc_scr[n] / l_scr[n]
        for g in range(A_GROUP):
            hh = n * A_GROUP + g
            o_ref[:, hh * A_HEAD_DIM:(hh + 1) * A_HEAD_DIM] = o[g * tq:(g + 1) * tq].astype(bf16)


def _dsa(proj, row0, nbatch, t_len, kit, kt, v, *, past, n_keys, topk):
    lp = v.shape[1]
    tq = _pick(t_len, (128, 64))
    tk = 256
    assert lp % tk == 0 and row0 % tq == 0 and past % CHUNK == 0
    nq = t_len // tq
    rb = row0 // tq

    def qmap(width_col):
        return lambda b, i: (rb + b * nq + i, width_col)

    return pl.pallas_call(
        functools.partial(_dsa_kernel, tq=tq, tk=tk, past=past, n_keys=n_keys, topk=topk,
                          nbits=lp.bit_length()),
        grid=(nbatch, nq),
        in_specs=[pl.BlockSpec((tq, IDX_HEADS * IDX_DIM), qmap(COL_QI // (IDX_HEADS * IDX_DIM))),
                  pl.BlockSpec((tq, SM_WIDTH), qmap(COL_SM // SM_WIDTH)),
                  pl.BlockSpec((tq, A_WIDTH), qmap(COL_QA // A_WIDTH)),
                  pl.BlockSpec((None, IDX_DIM, lp), lambda b, i: (b, 0, 0)),
                  pl.BlockSpec((None, KV_WIDTH, lp), lambda b, i: (b, 0, 0)),
                  pl.BlockSpec((None, lp, KV_WIDTH), lambda b, i: (b, 0, 0))],
        out_specs=pl.BlockSpec((tq, A_WIDTH), lambda b, i: (b * nq + i, 0)),
        out_shape=jax.ShapeDtypeStruct((nbatch * t_len, A_WIDTH), bf16),
        scratch_shapes=[pltpu.VMEM((tq, lp), i32),
                        pltpu.VMEM((A_KV_HEADS, A_GROUP * tq, A_HEAD_DIM), bf16),
                        pltpu.VMEM((A_KV_HEADS, A_GROUP * tq, 1), f32),
                        pltpu.VMEM((A_KV_HEADS, A_GROUP * tq, 1), f32),
                        pltpu.VMEM((A_KV_HEADS, A_GROUP * tq, A_HEAD_DIM), f32)],
        compiler_params=_params("arbitrary", "arbitrary"),
        name="dsa",
    )(proj, proj, proj, kit, kt, v)


def _hdot(a, b, dims=(((1,), (0,)), ((), ()))):
    return lax.dot_general(a, b, dims, precision=lax.Precision.HIGHEST, preferred_element_type=f32)


_NT = (((1,), (1,)), ((), ()))
_TN = (((0,), (0,)), ((), ()))


def _gdn_kernel(qkv_ref, sm_ref, z_ref, cw_ref, alog_ref, dtb_ref, gn_ref, buf_ref, s0_ref,
                ob_ref, sout_ref, xp_scr, s_scr, *, nc):
    c = pl.program_id(1)

    @pl.when(c == 0)
    def _():
        xp_scr[0:8, :] = buf_ref[...]
        s_scr[...] = s0_ref[...]

    x = qkv_ref[...]
    xp_scr[8:8 + CHUNK, :] = x
    cw = cw_ref[...]
    y = (cw[0:1] * xp_scr[5:5 + CHUNK, :] + cw[1:2] * xp_scr[6:6 + CHUNK, :]
         + cw[2:3] * xp_scr[7:7 + CHUNK, :] + cw[3:4] * x)
    xp_scr[0:8, :] = x[CHUNK - 8:CHUNK, :]
    y = y * jax.nn.sigmoid(y)

    sm = sm_ref[...]
    z = sm + dtb_ref[...]
    softplus = jnp.maximum(z, 0.0) + jnp.log1p(jnp.exp(-jnp.abs(z)))
    g_all = -jnp.exp(alog_ref[...]) * softplus
    beta_all = jax.nn.sigmoid(sm)
    r = lax.broadcasted_iota(i32, (CHUNK, CHUNK), 0)
    cc = lax.broadcasted_iota(i32, (CHUNK, CHUNK), 1)
    tri, strict = r >= cc, r > cc
    eye = jnp.where(r == cc, 1.0, 0.0)
    gc = _hdot(jnp.where(tri, 1.0, 0.0), g_all)
    gc_t = gc.T
    g_last = gc[CHUNK - 1:CHUNK, :]
    e_gc, e_last, e_rest = jnp.exp(gc), jnp.exp(g_last), jnp.exp(g_last - gc)
    gn = gn_ref[...]

    for h in range(B_HEADS):
        lg, lb = SM_AB + h, SM_BB + h
        hs = slice(h * B_KEY_DIM, (h + 1) * B_KEY_DIM)
        q = y[:, hs]
        k = y[:, B_QK_WIDTH + h * B_KEY_DIM:B_QK_WIDTH + (h + 1) * B_KEY_DIM]
        v = y[:, 2 * B_QK_WIDTH + h * B_VAL_DIM:2 * B_QK_WIDTH + (h + 1) * B_VAL_DIM]
        q = q * lax.rsqrt(jnp.sum(q * q, axis=-1, keepdims=True) + EPS) * (B_KEY_DIM ** -0.5)
        k = k * lax.rsqrt(jnp.sum(k * k, axis=-1, keepdims=True) + EPS)
        beta = beta_all[:, lb:lb + 1]
        diff = gc[:, lg:lg + 1] - gc_t[lg:lg + 1, :]
        decay = jnp.where(tri, jnp.exp(jnp.where(tri, diff, 0.0)), 0.0)
        kbeta = k * beta
        low = jnp.where(strict, _hdot(kbeta, k, _NT) * decay, 0.0)
        xpow, ainv = low, eye - low
        for _ in range(CHUNK.bit_length() - 2):
            xpow = _hdot(xpow, xpow)
            ainv = ainv + _hdot(ainv, xpow)
        rhs = jnp.concatenate([v * beta, kbeta * e_gc[:, lg:lg + 1]], axis=1)
        sol = _hdot(ainv, rhs)
        u, wmat = sol[:, :B_VAL_DIM], sol[:, B_VAL_DIM:]
        qk = _hdot(q, k, _NT) * decay
        s = s_scr[h]
        v_new = u - _hdot(wmat, s)
        o = _hdot(q * e_gc[:, lg:lg + 1], s) + _hdot(qk, v_new)
        s_scr[h] = s * e_last[:, lg:lg + 1] + _hdot(k * e_rest[:, lg:lg + 1], v_new, _TN)
        zh = z_ref[:, hs]
        ob_ref[:, hs] = (_rms(o, gn) * (zh * jax.nn.sigmoid(zh))).astype(bf16)

    @pl.when(c == nc - 1)
    def _():
        sout_ref[...] = s_scr[...]


def _gdn(proj, row0, nbatch, t_len, conv_w, alog_row, dtb_row, gn, buf8, s0):
    assert t_len % CHUNK == 0 and row0 % CHUNK == 0
    nc = t_len // CHUNK
    rb = row0 // CHUNK

    def rmap(col):
        return lambda b, c: (rb + b * nc + c, col)

    return pl.pallas_call(
        functools.partial(_gdn_kernel, nc=nc),
        grid=(nbatch, nc),
        in_specs=[pl.BlockSpec((CHUNK, CONV_CH), rmap(COL_QKVB // CONV_CH)),
                  pl.BlockSpec((CHUNK, SM_WIDTH), rmap(COL_SM // SM_WIDTH)),
                  pl.BlockSpec((CHUNK, B_WIDTH), rmap(COL_ZB // B_WIDTH)),
                  pl.BlockSpec((CONV_WIDTH, CONV_CH), lambda b, c: (0, 0)),
                  pl.BlockSpec((1, SM_WIDTH), lambda b, c: (0, 0)),
                  pl.BlockSpec((1, SM_WIDTH), lambda b, c: (0, 0)),
                  pl.BlockSpec((1, B_VAL_DIM), lambda b, c: (0, 0)),
                  pl.BlockSpec((None, 8, CONV_CH), lambda b, c: (b, 0, 0)),
                  pl.BlockSpec((None, B_HEADS, B_KEY_DIM, B_VAL_DIM), lambda b, c: (b, 0, 0, 0))],
        out_specs=[pl.BlockSpec((CHUNK, B_WIDTH), lambda b, c: (b * nc + c, 0)),
                   pl.BlockSpec((None, B_HEADS, B_KEY_DIM, B_VAL_DIM), lambda b, c: (b, 0, 0, 0))],
        out_shape=[jax.ShapeDtypeStruct((nbatch * t_len, B_WIDTH), bf16),
                   jax.ShapeDtypeStruct((nbatch, B_HEADS, B_KEY_DIM, B_VAL_DIM), f32)],
        scratch_shapes=[pltpu.VMEM((8 + CHUNK, CONV_CH), f32),
                        pltpu.VMEM((B_HEADS, B_KEY_DIM, B_VAL_DIM), f32)],
        compiler_params=_params("arbitrary", "arbitrary"),
        name="gdn",
    )(proj, proj, proj, conv_w, alog_row, dtb_row, gn, buf8, s0)


def _merge_kernel(oa_ref, ob_ref, wa_ref, wb_ref, ga_ref, gb_ref, o_ref):
    a = jnp.dot(oa_ref[...], wa_ref[...], preferred_element_type=f32)
    b = jnp.dot(ob_ref[...], wb_ref[...], preferred_element_type=f32)
    o_ref[...] = (jax.nn.sigmoid(ga_ref[...]) * a + jax.nn.sigmoid(gb_ref[...]) * b).astype(bf16)


def _merge(oa, ob, wa, wb, proj):
    m = oa.shape[0]
    d = wa.shape[1]
    tm = _pick(m, (1024, 512, 256, 128, 64))
    tn = 512
    assert d % tn == 0 and COL_GA % tn == 0 and COL_GB % tn == 0
    return pl.pallas_call(
        _merge_kernel,
        grid=(m // tm, d // tn),
        in_specs=[pl.BlockSpec((tm, A_WIDTH), lambda i, j: (i, 0)),
                  pl.BlockSpec((tm, B_WIDTH), lambda i, j: (i, 0)),
                  pl.BlockSpec((A_WIDTH, tn), lambda i, j: (0, j)),
                  pl.BlockSpec((B_WIDTH, tn), lambda i, j: (0, j)),
                  pl.BlockSpec((tm, tn), lambda i, j: (i, COL_GA // tn + j)),
                  pl.BlockSpec((tm, tn), lambda i, j: (i, COL_GB // tn + j))],
        out_specs=pl.BlockSpec((tm, tn), lambda i, j: (i, j)),
        out_shape=jax.ShapeDtypeStruct((m, d), bf16),
        compiler_params=_params("arbitrary", "arbitrary"),
        name="merge",
    )(oa, ob, wa, wb, proj, proj)


def _reorder_w_in(w):
    o, pts = 0, {}
    for name, width in (("qa", A_WIDTH), ("ka", KV_WIDTH), ("va", KV_WIDTH), ("qi", IDX_HEADS * IDX_DIM),
                        ("ki", IDX_DIM), ("wi", IDX_HEADS), ("qb", B_QK_WIDTH), ("kb", B_QK_WIDTH),
                        ("vb", B_WIDTH), ("zb", B_WIDTH), ("ab", B_HEADS), ("bb", B_HEADS)):
        pts[name] = (o, o + width)
        o += width
    d = w.shape[0]
    rest = w.shape[1] - o
    assert rest == 2 * d
    seg = lambda a, b: w[:, pts[a][0]:pts[b][1]]
    used = IDX_DIM + IDX_HEADS + 2 * B_HEADS
    out = jnp.concatenate([seg("qb", "vb"), seg("qa", "qa"), w[:, o:o + d], w[:, o + d:o + 2 * d], seg("zb", "zb"),
                           seg("ka", "va"), seg("qi", "qi"), seg("ki", "wi"), seg("ab", "bb"),
                           jnp.zeros((d, SM_WIDTH - used), w.dtype)], axis=1)
    assert out.shape[1] == PROJ_COLS and COL_GA == COL_QA + A_WIDTH and COL_GB == COL_GA + d
    return out.astype(bf16)


def _lane_row(vals, start):
    return jnp.zeros((1, SM_WIDTH), f32).at[0, start:start + vals.shape[0]].set(vals.astype(f32))


def kernel(x_prompt, x_sample, cache_k, cache_v, cache_kidx, state_conv, state_gdn, c_prompt, c_sample, w_ada, b_ada, norm_g, w1_gate, w1_up, w1_down, w_in, idx_ln_g, idx_ln_b, conv_w, a_log, dt_bias, gdn_norm_g, w_branch_a, w_branch_b, w_out, w2_gate, w2_up, w2_down):
    bp, tp, d = x_prompt.shape
    bs, ts, _ = x_sample.shape
    depth = w_ada.shape[0]
    past = cache_k.shape[2]
    assert tp % CHUNK == 0 and ts % CHUNK == 0 and ts >= CONV_WIDTH - 1 and tp >= CONV_WIDTH - 1
    mp, ms = bp * tp, bs * ts
    ngp, ngs = mp // CHUNK, ms // CHUNK

    x = jnp.concatenate([x_prompt.reshape(mp, d), x_sample.reshape(ms, d)], axis=0)
    c_all = jnp.concatenate([c_prompt, c_sample], axis=0)
    c_pad = jnp.pad(c_all, ((0, -c_all.shape[0] % 8), (0, 0)))

    outs_p, outs_s = [], []
    for l in range(depth):
        mod = _ada(c_pad, w_ada[l], b_ada[l][None])[:bp + bs].reshape(bp + bs, N_MOD, d)
        modg = jnp.concatenate([jnp.repeat(mod[:bp], tp // CHUNK, axis=0),
                                jnp.repeat(mod[bp:], ts // CHUNK, axis=0)], axis=0)
        ng = lambda i: norm_g[l, i][None]

        h = _norm_mod(x, modg, ng(0), 0, 1)
        act = _gateup(h, w1_gate[l].astype(bf16), w1_up[l].astype(bf16))
        x = _mm_norm_res(act, w1_down[l].astype(bf16), x, modg, ng(1), 2, FFN_WEIGHT)

        h = _norm_mod(x, modg, ng(2), 3, 4)
        proj = _mm(h, _reorder_w_in(w_in[l]))
        kidx = _kidx(proj, idx_ln_g[l][None], idx_ln_b[l][None])

        kv_p = proj[:mp, COL_KV:COL_KV + 2 * KV_WIDTH].reshape(bp, tp, 2 * KV_WIDTH)
        k_p, v_p = kv_p[..., :KV_WIDTH], kv_p[..., KV_WIDTH:]
        ki_p = kidx[:mp].reshape(bp, tp, IDX_DIM)
        kv_s = proj[mp:, COL_KV:COL_KV + 2 * KV_WIDTH].reshape(bs, ts, 2 * KV_WIDTH)
        k_s, v_s = kv_s[..., :KV_WIDTH], kv_s[..., KV_WIDTH:]
        ki_s = kidx[mp:].reshape(bs, ts, IDX_DIM)

        def keys(k_all, v_all, ki_all):
            n = k_all.shape[1]
            pad = ((0, 0), (0, -n % 256), (0, 0))
            kt = jnp.swapaxes(jnp.pad(k_all.astype(bf16), pad), 1, 2)
            kit = jnp.swapaxes(jnp.pad(ki_all.astype(bf16), pad), 1, 2)
            return kit, kt, jnp.pad(v_all.astype(bf16), pad), n

        kit, kt, vv, n_keys = keys(k_p, v_p, ki_p)
        oa_p = _dsa(proj, 0, bp, tp, kit, kt, vv, past=0, n_keys=n_keys, topk=min(TOPK_MAX, n_keys // 4))
        kit, kt, vv, n_keys = keys(
            jnp.concatenate([cache_k[l].reshape(bs, past, KV_WIDTH), k_s], axis=1),
            jnp.concatenate([cache_v[l].reshape(bs, past, KV_WIDTH), v_s], axis=1),
            jnp.concatenate([cache_kidx[l], ki_s], axis=1))
        oa_s = _dsa(proj, mp, bs, ts, kit, kt, vv, past=past, n_keys=n_keys, topk=min(TOPK_MAX, n_keys // 4))

        alog_row, dtb_row = _lane_row(a_log[l], SM_AB), _lane_row(dt_bias[l], SM_AB)
        gn = gdn_norm_g[l][None]
        ob_p, gdn_p = _gdn(proj, 0, bp, tp, conv_w[l], alog_row, dtb_row, gn,
                           jnp.zeros((bp, 8, CONV_CH), f32), jnp.zeros((bp, B_HEADS, B_KEY_DIM, B_VAL_DIM), f32))
        buf8 = jnp.pad(state_conv[l], ((0, 0), (8 - (CONV_WIDTH - 1), 0), (0, 0)))
        ob_s, gdn_s = _gdn(proj, mp, bs, ts, conv_w[l], alog_row, dtb_row, gn, buf8, state_gdn[l])

        merged = _merge(jnp.concatenate([oa_p, oa_s], axis=0), jnp.concatenate([ob_p, ob_s], axis=0),
                        w_branch_a[l].astype(bf16), w_branch_b[l].astype(bf16), proj)
        x = _mm_norm_res(merged, w_out[l].astype(bf16), x, modg, ng(3), 5, None)

        h = _norm_mod(x, modg, ng(4), 6, 7)
        act = _gateup(h, w2_gate[l].astype(bf16), w2_up[l].astype(bf16))
        x = _mm_norm_res(act, w2_down[l].astype(bf16), x, modg, ng(5), 8, FFN_WEIGHT)

        tail = CONV_WIDTH - 1
        conv_p = proj[:mp, :CONV_CH].reshape(bp, tp, CONV_CH)[:, tp - tail:]
        conv_s = proj[mp:, :CONV_CH].reshape(bs, ts, CONV_CH)[:, ts - tail:]
        outs_p.append((k_p.reshape(bp, tp, A_KV_HEADS, A_HEAD_DIM), v_p.reshape(bp, tp, A_KV_HEADS, A_HEAD_DIM),
                       ki_p, conv_p, gdn_p))
        outs_s.append((k_s.reshape(bs, ts, A_KV_HEADS, A_HEAD_DIM), v_s.reshape(bs, ts, A_KV_HEADS, A_HEAD_DIM),
                       ki_s, conv_s, gdn_s))

    stack = lambda outs: [jnp.stack([o[i] for o in outs], axis=0) for i in range(5)]
    return (x[:mp].reshape(bp, tp, d), x[mp:].reshape(bs, ts, d), *stack(outs_p), *stack(outs_s))
```

```python
import functools

import jax
import jax.numpy as jnp
from jax import lax
from jax.experimental import pallas as pl
from jax.experimental.pallas import tpu as pltpu

f32 = jnp.float32
bf16 = jnp.bfloat16
i32 = jnp.int32
i16 = jnp.int16

CHUNK = 64
A_HEADS, A_KV_HEADS, A_HEAD_DIM = 8, 2, 128
A_GROUP = A_HEADS // A_KV_HEADS
A_WIDTH = A_HEADS * A_HEAD_DIM
KV_WIDTH = A_KV_HEADS * A_HEAD_DIM
IDX_HEADS, IDX_DIM = 8, 64
TOPK_MAX = 256
B_HEADS, B_KEY_DIM, B_VAL_DIM = 8, 128, 128
B_QK_WIDTH = B_HEADS * B_KEY_DIM
B_WIDTH = B_HEADS * B_VAL_DIM
CONV_WIDTH = 4
CONV_CH = 2 * B_QK_WIDTH + B_WIDTH
N_MOD = 9
FFN_WEIGHT = 0.5
EPS = 1e-6

COL_QKVB, COL_QA, COL_GA, COL_GB, COL_ZB, COL_KV, COL_QI, COL_SM = 0, 3072, 4096, 6144, 8192, 9216, 9728, 10240
SM_WIDTH = 128
SM_WI, SM_AB, SM_BB = 64, 72, 80
PROJ_COLS = COL_SM + SM_WIDTH

INT_MIN = -(2 ** 31)
INT_MAX = 2 ** 31 - 1
I16_MIN, I16_MAX = -(2 ** 15), 2 ** 15 - 1
NEG = -1e30
MASKED_BF16_BITS = -3638
LANES = 128
LOG2E = 1.4426950408889634
DSA_TK_PROMPT, DSA_TK_SAMPLE = 512, 256
ATT_ROWS = 64
VMEM_LIMIT = 56 * 1024 * 1024


def _pick(n, prefs):
    for p in prefs:
        if n % p == 0:
            return p
    raise ValueError(f"no tile of {prefs} divides {n}")


def _params(*sem):
    return pltpu.CompilerParams(dimension_semantics=sem, vmem_limit_bytes=VMEM_LIMIT)


def _rms(y, g):
    return y * lax.rsqrt(jnp.mean(y * y, axis=-1, keepdims=True) + EPS) * g


def _ada_kernel(c_ref, w_ref, b_ref, o_ref):
    o_ref[...] = jnp.dot(c_ref[...].astype(bf16), w_ref[...].astype(bf16),
                         preferred_element_type=f32) + b_ref[...]


def _ada(c, w, b):
    m, d = c.shape
    n = w.shape[1]
    tn = _pick(n, (1024, 512, 256, 128))
    return pl.pallas_call(
        _ada_kernel,
        grid=(n // tn,),
        in_specs=[pl.BlockSpec((m, d), lambda j: (0, 0)),
                  pl.BlockSpec((d, tn), lambda j: (0, j)),
                  pl.BlockSpec((1, tn), lambda j: (0, j))],
        out_specs=pl.BlockSpec((m, tn), lambda j: (0, j)),
        out_shape=jax.ShapeDtypeStruct((m, n), f32),
        compiler_params=_params("arbitrary"),
        name="ada",
    )(c, w, b)


def _norm_mod_kernel(x_ref, mod_ref, g_ref, o_ref, *, sh, sc, nb):
    g = g_ref[...]
    for j in range(nb):
        rows = slice(j * CHUNK, (j + 1) * CHUNK)
        y = _rms(x_ref[rows, :], g)
        o_ref[rows, :] = (y * (1.0 + mod_ref[j, sc:sc + 1, :]) + mod_ref[j, sh:sh + 1, :]).astype(bf16)


def _norm_mod(x, modg, g, sh, sc):
    m, d = x.shape
    tm = _pick(m, (512, 256, 128, 64))
    nb = tm // CHUNK
    return pl.pallas_call(
        functools.partial(_norm_mod_kernel, sh=sh, sc=sc, nb=nb),
        grid=(m // tm,),
        in_specs=[pl.BlockSpec((tm, d), lambda i: (i, 0)),
                  pl.BlockSpec((nb, N_MOD, d), lambda i: (i, 0, 0)),
                  pl.BlockSpec((1, d), lambda i: (0, 0))],
        out_specs=pl.BlockSpec((tm, d), lambda i: (i, 0)),
        out_shape=jax.ShapeDtypeStruct((m, d), bf16),
        compiler_params=_params("arbitrary"),
        name="norm_mod",
    )(x, modg, g)


def _gateup_kernel(h_ref, wg_ref, wu_ref, o_ref):
    h = h_ref[...]
    a = jnp.dot(h, wg_ref[...], preferred_element_type=f32)
    u = jnp.dot(h, wu_ref[...], preferred_element_type=f32)
    o_ref[...] = (a * jax.nn.sigmoid(a) * u).astype(bf16)


def _gateup(h, wg, wu):
    m, d = h.shape
    n = wg.shape[1]
    tm = _pick(m, (1024, 512, 256, 128, 64))
    tn = _pick(n, (512, 256, 128))
    return pl.pallas_call(
        _gateup_kernel,
        grid=(m // tm, n // tn),
        in_specs=[pl.BlockSpec((tm, d), lambda i, j: (i, 0)),
                  pl.BlockSpec((d, tn), lambda i, j: (0, j)),
                  pl.BlockSpec((d, tn), lambda i, j: (0, j))],
        out_specs=pl.BlockSpec((tm, tn), lambda i, j: (i, j)),
        out_shape=jax.ShapeDtypeStruct((m, n), bf16),
        compiler_params=_params("arbitrary", "arbitrary"),
        name="gateup",
    )(h, wg, wu)


def _mm_norm_res_kernel(a_ref, w_ref, x_ref, mod_ref, g_ref, o_ref, acc_ref, *, gt, coeff, nb, nk):
    k = pl.program_id(1)

    @pl.when(k == 0)
    def _():
        acc_ref[...] = jnp.zeros_like(acc_ref)

    acc_ref[...] += jnp.dot(a_ref[...], w_ref[...], preferred_element_type=f32)

    @pl.when(k == nk - 1)
    def _():
        g = g_ref[...]
        for j in range(nb):
            rows = slice(j * CHUNK, (j + 1) * CHUNK)
            gate = mod_ref[j, gt:gt + 1, :]
            if coeff is not None:
                gate = coeff * gate
            o_ref[rows, :] = x_ref[rows, :] + gate * _rms(acc_ref[rows, :], g)


def _mm_norm_res(a, w, x, modg, g, gt, coeff):
    m, kd = a.shape
    d = w.shape[1]
    tm = _pick(m, (512, 256, 128, 64))
    tk = _pick(kd, (512, 256, 128))
    nb, nk = tm // CHUNK, kd // tk
    return pl.pallas_call(
        functools.partial(_mm_norm_res_kernel, gt=gt, coeff=coeff, nb=nb, nk=nk),
        grid=(m // tm, nk),
        in_specs=[pl.BlockSpec((tm, tk), lambda i, k: (i, k)),
                  pl.BlockSpec((tk, d), lambda i, k: (k, 0)),
                  pl.BlockSpec((tm, d), lambda i, k: (i, 0)),
                  pl.BlockSpec((nb, N_MOD, d), lambda i, k: (i, 0, 0)),
                  pl.BlockSpec((1, d), lambda i, k: (0, 0))],
        out_specs=pl.BlockSpec((tm, d), lambda i, k: (i, 0)),
        out_shape=jax.ShapeDtypeStruct((m, d), f32),
        scratch_shapes=[pltpu.VMEM((tm, d), f32)],
        compiler_params=_params("arbitrary", "arbitrary"),
        name="mm_norm_res",
    )(a, w, x, modg, g)


def _mm_kernel(a_ref, w_ref, o_ref):
    o_ref[...] = jnp.dot(a_ref[...], w_ref[...], preferred_element_type=f32)


def _mm(a, w):
    m, kd = a.shape
    n = w.shape[1]
    tm = _pick(m, (1024, 512, 256, 128, 64))
    tn = _pick(n, (1152, 1024, 512, 384, 256, 128))
    return pl.pallas_call(
        _mm_kernel,
        grid=(m // tm, n // tn),
        in_specs=[pl.BlockSpec((tm, kd), lambda i, j: (i, 0)),
                  pl.BlockSpec((kd, tn), lambda i, j: (0, j))],
        out_specs=pl.BlockSpec((tm, tn), lambda i, j: (i, j)),
        out_shape=jax.ShapeDtypeStruct((m, n), f32),
        compiler_params=_params("arbitrary", "arbitrary"),
        name="in_proj",
    )(a, w)


def _kidx_kernel(sm_ref, g_ref, b_ref, o_ref):
    x = sm_ref[:, 0:IDX_DIM]
    xc = x - jnp.mean(x, axis=-1, keepdims=True)
    y = xc * lax.rsqrt(jnp.mean(xc * xc, axis=-1, keepdims=True) + EPS)
    o_ref[...] = y * g_ref[...] + b_ref[...]


def _kidx(proj, g, b):
    m = proj.shape[0]
    tm = _pick(m, (1024, 512, 256, 128, 64))
    return pl.pallas_call(
        _kidx_kernel,
        grid=(m // tm,),
        in_specs=[pl.BlockSpec((tm, SM_WIDTH), lambda i: (i, COL_SM // SM_WIDTH)),
                  pl.BlockSpec((1, IDX_DIM), lambda i: (0, 0)),
                  pl.BlockSpec((1, IDX_DIM), lambda i: (0, 0))],
        out_specs=pl.BlockSpec((tm, IDX_DIM), lambda i: (i, 0)),
        out_shape=jax.ShapeDtypeStruct((m, IDX_DIM), f32),
        compiler_params=_params("arbitrary"),
        name="kidx_ln",
    )(proj, g, b)


def _dsa_kernel(qi_ref, sm_ref, qa_ref, kit_ref, kt_ref, v_ref, o_ref,
                hi_ref, lo_ref, w_scr, q_scr, p_scr, m_scr, l_scr, acc_scr, *, tq, tk, past, n_keys, topk, nbits):
    chunk_shift = CHUNK.bit_length() - 1
    nlc = tk // LANES
    qpos0 = past + pl.program_id(1) * tq
    limit = jnp.minimum((((qpos0 + tq - 1) >> chunk_shift) + 1) * CHUNK, n_keys)
    nkb = (limit + tk - 1) // tk
    row = lax.broadcasted_iota(i32, (tq, LANES), 0)
    lane = lax.broadcasted_iota(i32, (tq, LANES), 1)
    qchunk = (qpos0 + row) >> chunk_shift

    def lane_chunks(kb):
        c0 = pl.multiple_of(kb * tk, tk)
        return c0, [pl.multiple_of(c0 + c * LANES, LANES) for c in range(nlc)]

    qi = qi_ref[...].astype(bf16)
    qis = [qi[:, h * IDX_DIM:(h + 1) * IDX_DIM] for h in range(IDX_HEADS)]
    w = sm_ref[:, SM_WI:SM_WI + IDX_HEADS] * (IDX_HEADS ** -0.5 * IDX_DIM ** -0.5)
    for h in range(IDX_HEADS):
        w_scr[h] = jnp.broadcast_to(w[:, h:h + 1], (tq, LANES))

    def score_body(masked, kb, carry):
        c0, ccs = lane_chunks(kb)
        kit = kit_ref[:, pl.ds(c0, tk)]
        sc = [jnp.zeros((tq, LANES), f32)] * nlc
        for h in range(IDX_HEADS):
            s = jnp.dot(qis[h], kit, preferred_element_type=f32)
            wh = w_scr[h]
            sc = [sc[c] + wh * jnp.maximum(s[:, c * LANES:(c + 1) * LANES], 0.0) for c in range(nlc)]
        for c, cc in enumerate(ccs):
            bits = lax.bitcast_convert_type(sc[c], i32)
            key = jnp.where(bits < 0, INT_MIN - bits, bits)
            if masked:
                kpos = cc + lane
                allowed = ((kpos >> chunk_shift) <= qchunk) & (kpos < n_keys)
                key = jnp.where(allowed, key, INT_MIN)
            hi_ref[:, pl.ds(cc, LANES)] = (key >> 16).astype(i16)
            lo_ref[:, pl.ds(cc, LANES)] = (key ^ 0x8000).astype(i16)
        return carry

    nfull = jnp.minimum((qpos0 >> chunk_shift) * CHUNK, n_keys) // tk
    lax.fori_loop(0, nfull, functools.partial(score_body, False), 0)
    lax.fori_loop(nfull, nkb, functools.partial(score_body, True), 0)

    def count16(pred):
        def body(kb, acc):
            _, ccs = lane_chunks(kb)
            for cc in ccs:
                acc = acc + jnp.where(pred(cc), jnp.int16(1), jnp.int16(0))
            return acc
        acc = lax.fori_loop(0, nkb, body, jnp.zeros((tq, LANES), i16))
        return jnp.broadcast_to(jnp.sum(acc.astype(f32), axis=1, keepdims=True), (tq, LANES))

    hi_at = lambda cc: hi_ref[:, pl.ds(cc, LANES)]
    lo_at = lambda cc: lo_ref[:, pl.ds(cc, LANES)]
    kpos16 = lambda cc: (cc + lane).astype(i16)
    kf = float(topk)

    def bisect16(at, ct0):
        def bit_body(it, carry):
            t, ct = carry
            cand = t + lax.shift_left(jnp.int32(1), 15 - it)
            c16 = cand.astype(i16)
            cnt = count16(lambda cc: at(cc) >= c16)
            take = cnt >= kf
            return jnp.where(take, cand, t), jnp.where(take, cnt, ct)
        return lax.fori_loop(0, 16, bit_body, (jnp.full((tq, LANES), I16_MIN, i32), ct0))

    th, ct = bisect16(hi_at, jnp.zeros((tq, LANES), f32) + (nkb * tk).astype(f32))
    th16 = th.astype(i16)

    def mask_body(kb, carry):
        _, ccs = lane_chunks(kb)
        for cc in ccs:
            hi = hi_at(cc)
            lo_ref[:, pl.ds(cc, LANES)] = jnp.where(hi > th16, jnp.int16(I16_MAX),
                                                    jnp.where(hi == th16, lo_at(cc), jnp.int16(I16_MIN)))
        return carry

    lax.fori_loop(0, nkb, mask_body, 0)
    tl, ct = bisect16(lo_at, ct)
    tl16 = tl.astype(i16)

    def tie_cut():
        r = kf - count16(lambda cc: (hi_at(cc) > th16) | ((hi_at(cc) == th16) & (lo_at(cc) > tl16)))

        def j_body(it, j):
            cand = j + lax.shift_left(jnp.int32(1), nbits - 1 - it)
            c16 = cand.astype(i16)
            cnt = count16(lambda cc: (hi_at(cc) == th16) & (lo_at(cc) == tl16) & (kpos16(cc) < c16))
            return jnp.where(cnt < r, cand, j)

        return lax.fori_loop(0, nbits, j_body, jnp.zeros((tq, LANES), i32))

    jcut = lax.cond(jnp.max(ct) > kf, tie_cut, lambda: jnp.full((tq, LANES), I16_MAX, i32)).astype(i16)
    tls16 = jnp.where(th == I16_MIN, jnp.maximum(tl, I16_MIN + 1), tl).astype(i16)

    def bias_body(kb, carry):
        _, ccs = lane_chunks(kb)
        for cc in ccs:
            hi, lo = hi_at(cc), lo_at(cc)
            sel = (hi > th16) | ((hi == th16) & ((lo > tls16) | ((lo == tls16) & (kpos16(cc) <= jcut))))
            hi_ref[:, pl.ds(cc, LANES)] = jnp.where(sel, jnp.int16(0), jnp.int16(MASKED_BF16_BITS))
        return carry

    lax.fori_loop(0, nkb, bias_body, 0)

    qa = qa_ref[...] * (A_HEAD_DIM ** -0.5 * LOG2E)
    for n in range(A_KV_HEADS):
        q_scr[n] = jnp.concatenate(
            [qa[:, (n * A_GROUP + g) * A_HEAD_DIM:(n * A_GROUP + g + 1) * A_HEAD_DIM] for g in range(A_GROUP)],
            axis=0).astype(bf16)
    m_scr[...] = jnp.full(m_scr.shape, NEG, f32)
    l_scr[...] = jnp.zeros(l_scr.shape, f32)
    acc_scr[...] = jnp.zeros(acc_scr.shape, f32)

    def att_body(kb, carry):
        c0, ccs = lane_chunks(kb)
        bias = [lax.bitcast_convert_type(hi_at(cc), bf16).astype(f32) for cc in ccs]
        for n in range(A_KV_HEADS):
            kt = kt_ref[n * A_HEAD_DIM:(n + 1) * A_HEAD_DIM, pl.ds(c0, tk)]
            vv = v_ref[pl.ds(c0, tk), n * A_HEAD_DIM:(n + 1) * A_HEAD_DIM]
            s_all = jnp.dot(q_scr[n], kt, preferred_element_type=f32)
            alphas = []
            for blk in range(A_GROUP * tq // ATT_ROWS):
                rows = slice(blk * ATT_ROWS, (blk + 1) * ATT_ROWS)
                qrows = slice(blk * ATT_ROWS % tq, blk * ATT_ROWS % tq + ATT_ROWS)
                s = [s_all[rows, c * LANES:(c + 1) * LANES] + bias[c][qrows] for c in range(nlc)]
                mx = s[0]
                for c in range(1, nlc):
                    mx = jnp.maximum(mx, s[c])
                m_old = m_scr[n, rows]
                m_new = jnp.maximum(m_old, jnp.max(mx, axis=1, keepdims=True))
                psum = None
                for c in range(nlc):
                    p = jnp.exp2(s[c] - m_new)
                    p_scr[n, rows, c * LANES:(c + 1) * LANES] = p.astype(bf16)
                    psum = p if psum is None else psum + p
                alpha = jnp.exp2(m_old - m_new)
                l_scr[n, rows] = alpha * l_scr[n, rows] + jnp.sum(psum, axis=1, keepdims=True)
                m_scr[n, rows] = m_new
                alphas.append(alpha)
            acc_scr[n] = (jnp.concatenate(alphas, axis=0) * acc_scr[n]
                          + jnp.dot(p_scr[n], vv, preferred_element_type=f32))
        return carry

    lax.fori_loop(0, nkb, att_body, 0)

    for n in range(A_KV_HEADS):
        o = acc_scr[n] / l_scr[n]
        for g in range(A_GROUP):
            hh = n * A_GROUP + g
            o_ref[:, hh * A_HEAD_DIM:(hh + 1) * A_HEAD_DIM] = o[g * tq:(g + 1) * tq].astype(bf16)


def _dsa(proj, row0, nbatch, t_len, kit, kt, v, *, tk, past, n_keys, topk):
    lp = v.shape[1]
    tq = _pick(t_len, (128, 64))
    assert lp % tk == 0 and tk % LANES == 0 and row0 % tq == 0 and past % CHUNK == 0
    assert lp <= I16_MAX and tq % ATT_ROWS == 0
    nq = t_len // tq
    rb = row0 // tq
    rows = A_GROUP * tq

    def qmap(width_col):
        return lambda b, i: (rb + b * nq + i, width_col)

    return pl.pallas_call(
        functools.partial(_dsa_kernel, tq=tq, tk=tk, past=past, n_keys=n_keys, topk=topk,
                          nbits=lp.bit_length()),
        grid=(nbatch, nq),
        in_specs=[pl.BlockSpec((tq, IDX_HEADS * IDX_DIM), qmap(COL_QI // (IDX_HEADS * IDX_DIM))),
                  pl.BlockSpec((tq, SM_WIDTH), qmap(COL_SM // SM_WIDTH)),
                  pl.BlockSpec((tq, A_WIDTH), qmap(COL_QA // A_WIDTH)),
                  pl.BlockSpec((None, IDX_DIM, lp), lambda b, i: (b, 0, 0)),
                  pl.BlockSpec((None, KV_WIDTH, lp), lambda b, i: (b, 0, 0)),
                  pl.BlockSpec((None, lp, KV_WIDTH), lambda b, i: (b, 0, 0))],
        out_specs=pl.BlockSpec((tq, A_WIDTH), lambda b, i: (b * nq + i, 0)),
        out_shape=jax.ShapeDtypeStruct((nbatch * t_len, A_WIDTH), bf16),
        scratch_shapes=[pltpu.VMEM((tq, lp), i16),
                        pltpu.VMEM((tq, lp), i16),
                        pltpu.VMEM((IDX_HEADS, tq, LANES), f32),
                        pltpu.VMEM((A_KV_HEADS, rows, A_HEAD_DIM), bf16),
                        pltpu.VMEM((A_KV_HEADS, rows, tk), bf16),
                        pltpu.VMEM((A_KV_HEADS, rows, LANES), f32),
                        pltpu.VMEM((A_KV_HEADS, rows, LANES), f32),
                        pltpu.VMEM((A_KV_HEADS, rows, A_HEAD_DIM), f32)],
        compiler_params=_params("arbitrary", "arbitrary"),
        name="dsa",
    )(proj, proj, proj, kit, kt, v)


_NN = (((1,), (0,)), ((), ()))
_NT = (((1,), (1,)), ((), ()))
_TN = (((0,), (0,)), ((), ()))


def _split(a):
    hi = a.astype(bf16)
    return hi, (a - hi.astype(f32)).astype(bf16)


def _dot3(a, b, dims=_NN):
    ah, al = _split(a)
    bh, bl = _split(b)
    d = lambda x, y: lax.dot_general(x, y, dims, preferred_element_type=f32)
    return d(ah, bh) + (d(ah, bl) + d(al, bh))


GDN_GROUP = 4


def _gdn_kernel(qkv_ref, sm_ref, z_ref, cw_ref, alog_ref, dtb_ref, gn_ref, buf_ref, s0_ref,
                ob_ref, sout_ref, xp_scr, s_scr, *, nc):
    c = pl.program_id(1)
    chunk_shift = CHUNK.bit_length() - 1

    @pl.when(c == 0)
    def _():
        xp_scr[0:8, :] = buf_ref[...]
        s_scr[...] = s0_ref[...]

    x = qkv_ref[...]
    xp_scr[8:8 + CHUNK, :] = x
    cw = cw_ref[...]
    y = (cw[0:1] * xp_scr[5:5 + CHUNK, :] + cw[1:2] * xp_scr[6:6 + CHUNK, :]
         + cw[2:3] * xp_scr[7:7 + CHUNK, :] + cw[3:4] * x)
    xp_scr[0:8, :] = x[CHUNK - 8:CHUNK, :]
    y = y * jax.nn.sigmoid(y)

    sm = sm_ref[...]
    z = sm + dtb_ref[...]
    softplus = jnp.maximum(z, 0.0) + jnp.log1p(jnp.exp(-jnp.abs(z)))
    g_all = -jnp.exp(alog_ref[...]) * softplus
    beta_all = jax.nn.sigmoid(sm)
    r1 = lax.broadcasted_iota(i32, (CHUNK, CHUNK), 0)
    c1 = lax.broadcasted_iota(i32, (CHUNK, CHUNK), 1)
    gc = lax.dot_general(jnp.where(r1 >= c1, 1.0, 0.0), g_all, _NN,
                         precision=lax.Precision.HIGHEST, preferred_element_type=f32)
    gc_t = gc.T
    g_last = gc[CHUNK - 1:CHUNK, :]
    e_gc, e_last, e_rest = jnp.exp(gc), jnp.exp(g_last), jnp.exp(g_last - gc)
    gn = gn_ref[...]

    gr = GDN_GROUP * CHUNK
    r = lax.broadcasted_iota(i32, (gr, gr), 0)
    cc = lax.broadcasted_iota(i32, (gr, gr), 1)
    same = (r >> chunk_shift) == (cc >> chunk_shift)
    tri, strict = same & (r >= cc), same & (r > cc)
    eye = jnp.where(r == cc, 1.0, 0.0)

    for grp in range(B_HEADS // GDN_GROUP):
        heads = range(grp * GDN_GROUP, (grp + 1) * GDN_GROUP)
        qs, ks, kbs, vbs = [], [], [], []
        for h in heads:
            q = y[:, h * B_KEY_DIM:(h + 1) * B_KEY_DIM]
            k = y[:, B_QK_WIDTH + h * B_KEY_DIM:B_QK_WIDTH + (h + 1) * B_KEY_DIM]
            v = y[:, 2 * B_QK_WIDTH + h * B_VAL_DIM:2 * B_QK_WIDTH + (h + 1) * B_VAL_DIM]
            beta = beta_all[:, SM_BB + h:SM_BB + h + 1]
            qs.append(q * lax.rsqrt(jnp.sum(q * q, axis=-1, keepdims=True) + EPS) * (B_KEY_DIM ** -0.5))
            k = k * lax.rsqrt(jnp.sum(k * k, axis=-1, keepdims=True) + EPS)
            ks.append(k)
            kbs.append(k * beta)
            vbs.append(v * beta)
        q4, k4, kb4, vb4 = (jnp.concatenate(a, axis=0) for a in (qs, ks, kbs, vbs))
        col = lambda a: jnp.concatenate([a[:, SM_AB + h:SM_AB + h + 1] for h in heads], axis=0)
        gcol, egc = col(gc), col(e_gc)
        grow = jnp.concatenate([gc_t[SM_AB + h:SM_AB + h + 1, :] for h in heads], axis=1)
        decay = jnp.where(tri, jnp.exp(jnp.where(tri, gcol - grow, 0.0)), 0.0)
        low = jnp.where(strict, _dot3(kb4, k4, _NT) * decay, 0.0)
        xpow, ainv = low, eye - low
        for _ in range(chunk_shift - 1):
            xpow = _dot3(xpow, xpow)
            ainv = ainv + _dot3(ainv, xpow)
        sol = _dot3(ainv, jnp.concatenate([vb4, kb4 * egc], axis=1))
        u4, w4 = sol[:, :B_VAL_DIM], sol[:, B_VAL_DIM:]
        qk4 = _dot3(q4, k4, _NT) * decay
        qe4 = q4 * egc
        states = [s_scr[h] for h in heads]
        rows = [slice(j * CHUNK, (j + 1) * CHUNK) for j in range(GDN_GROUP)]
        v_new = [u4[rows[j]] - _dot3(w4[rows[j]], states[j]) for j in range(GDN_GROUP)]
        o4 = (jnp.concatenate([_dot3(qe4[rows[j]], states[j]) for j in range(GDN_GROUP)], axis=0)
              + _dot3(qk4, jnp.concatenate(v_new, axis=0)))
        for j, h in enumerate(heads):
            lg = SM_AB + h
            s_scr[h] = (states[j] * e_last[:, lg:lg + 1]
                        + _dot3(ks[j] * e_rest[:, lg:lg + 1], v_new[j], _TN))
            hs = slice(h * B_VAL_DIM, (h + 1) * B_VAL_DIM)
            zh = z_ref[:, hs]
            ob_ref[:, hs] = (_rms(o4[rows[j]], gn) * (zh * jax.nn.sigmoid(zh))).astype(bf16)

    @pl.when(c == nc - 1)
    def _():
        sout_ref[...] = s_scr[...]


def _gdn(proj, row0, nbatch, t_len, conv_w, alog_row, dtb_row, gn, buf8, s0):
    assert t_len % CHUNK == 0 and row0 % CHUNK == 0
    nc = t_len // CHUNK
    rb = row0 // CHUNK

    def rmap(col):
        return lambda b, c: (rb + b * nc + c, col)

    return pl.pallas_call(
        functools.partial(_gdn_kernel, nc=nc),
        grid=(nbatch, nc),
        in_specs=[pl.BlockSpec((CHUNK, CONV_CH), rmap(COL_QKVB // CONV_CH)),
                  pl.BlockSpec((CHUNK, SM_WIDTH), rmap(COL_SM // SM_WIDTH)),
                  pl.BlockSpec((CHUNK, B_WIDTH), rmap(COL_ZB // B_WIDTH)),
                  pl.BlockSpec((CONV_WIDTH, CONV_CH), lambda b, c: (0, 0)),
                  pl.BlockSpec((1, SM_WIDTH), lambda b, c: (0, 0)),
                  pl.BlockSpec((1, SM_WIDTH), lambda b, c: (0, 0)),
                  pl.BlockSpec((1, B_VAL_DIM), lambda b, c: (0, 0)),
                  pl.BlockSpec((None, 8, CONV_CH), lambda b, c: (b, 0, 0)),
                  pl.BlockSpec((None, B_HEADS, B_KEY_DIM, B_VAL_DIM), lambda b, c: (b, 0, 0, 0))],
        out_specs=[pl.BlockSpec((CHUNK, B_WIDTH), lambda b, c: (b * nc + c, 0)),
                   pl.BlockSpec((None, B_HEADS, B_KEY_DIM, B_VAL_DIM), lambda b, c: (b, 0, 0, 0))],
        out_shape=[jax.ShapeDtypeStruct((nbatch * t_len, B_WIDTH), bf16),
                   jax.ShapeDtypeStruct((nbatch, B_HEADS, B_KEY_DIM, B_VAL_DIM), f32)],
        scratch_shapes=[pltpu.VMEM((8 + CHUNK, CONV_CH), f32),
                        pltpu.VMEM((B_HEADS, B_KEY_DIM, B_VAL_DIM), f32)],
        compiler_params=_params("arbitrary", "arbitrary"),
        name="gdn",
    )(proj, proj, proj, conv_w, alog_row, dtb_row, gn, buf8, s0)


def _merge_kernel(oa_ref, ob_ref, wa_ref, wb_ref, ga_ref, gb_ref, o_ref):
    a = jnp.dot(oa_ref[...], wa_ref[...], preferred_element_type=f32)
    b = jnp.dot(ob_ref[...], wb_ref[...], preferred_element_type=f32)
    o_ref[...] = (jax.nn.sigmoid(ga_ref[...]) * a + jax.nn.sigmoid(gb_ref[...]) * b).astype(bf16)


def _merge(oa, ob, wa, wb, proj):
    m = oa.shape[0]
    d = wa.shape[1]
    tm = _pick(m, (1024, 512, 256, 128, 64))
    tn = 512
    assert d % tn == 0 and COL_GA % tn == 0 and COL_GB % tn == 0
    return pl.pallas_call(
        _merge_kernel,
        grid=(m // tm, d // tn),
        in_specs=[pl.BlockSpec((tm, A_WIDTH), lambda i, j: (i, 0)),
                  pl.BlockSpec((tm, B_WIDTH), lambda i, j: (i, 0)),
                  pl.BlockSpec((A_WIDTH, tn), lambda i, j: (0, j)),
                  pl.BlockSpec((B_WIDTH, tn), lambda i, j: (0, j)),
                  pl.BlockSpec((tm, tn), lambda i, j: (i, COL_GA // tn + j)),
                  pl.BlockSpec((tm, tn), lambda i, j: (i, COL_GB // tn + j))],
        out_specs=pl.BlockSpec((tm, tn), lambda i, j: (i, j)),
        out_shape=jax.ShapeDtypeStruct((m, d), bf16),
        compiler_params=_params("arbitrary", "arbitrary"),
        name="merge",
    )(oa, ob, wa, wb, proj, proj)


def _reorder_w_in(w):
    o, pts = 0, {}
    for name, width in (("qa", A_WIDTH), ("ka", KV_WIDTH), ("va", KV_WIDTH), ("qi", IDX_HEADS * IDX_DIM),
                        ("ki", IDX_DIM), ("wi", IDX_HEADS), ("qb", B_QK_WIDTH), ("kb", B_QK_WIDTH),
                        ("vb", B_WIDTH), ("zb", B_WIDTH), ("ab", B_HEADS), ("bb", B_HEADS)):
        pts[name] = (o, o + width)
        o += width
    d = w.shape[0]
    rest = w.shape[1] - o
    assert rest == 2 * d
    seg = lambda a, b: w[:, pts[a][0]:pts[b][1]]
    used = IDX_DIM + IDX_HEADS + 2 * B_HEADS
    out = jnp.concatenate([seg("qb", "vb"), seg("qa", "qa"), w[:, o:o + d], w[:, o + d:o + 2 * d], seg("zb", "zb"),
                           seg("ka", "va"), seg("qi", "qi"), seg("ki", "wi"), seg("ab", "bb"),
                           jnp.zeros((d, SM_WIDTH - used), w.dtype)], axis=1)
    assert out.shape[1] == PROJ_COLS and COL_GA == COL_QA + A_WIDTH and COL_GB == COL_GA + d
    return out.astype(bf16)


def _lane_row(vals, start):
    return jnp.zeros((1, SM_WIDTH), f32).at[0, start:start + vals.shape[0]].set(vals.astype(f32))


def kernel(x_prompt, x_sample, cache_k, cache_v, cache_kidx, state_conv, state_gdn, c_prompt, c_sample, w_ada, b_ada, norm_g, w1_gate, w1_up, w1_down, w_in, idx_ln_g, idx_ln_b, conv_w, a_log, dt_bias, gdn_norm_g, w_branch_a, w_branch_b, w_out, w2_gate, w2_up, w2_down):
    bp, tp, d = x_prompt.shape
    bs, ts, _ = x_sample.shape
    depth = w_ada.shape[0]
    past = cache_k.shape[2]
    assert tp % CHUNK == 0 and ts % CHUNK == 0 and ts >= CONV_WIDTH - 1 and tp >= CONV_WIDTH - 1
    mp, ms = bp * tp, bs * ts

    x = jnp.concatenate([x_prompt.reshape(mp, d), x_sample.reshape(ms, d)], axis=0)
    c_all = jnp.concatenate([c_prompt, c_sample], axis=0)
    c_pad = jnp.pad(c_all, ((0, -c_all.shape[0] % 8), (0, 0)))

    outs_p, outs_s = [], []
    for l in range(depth):
        mod = _ada(c_pad, w_ada[l], b_ada[l][None])[:bp + bs].reshape(bp + bs, N_MOD, d)
        modg = jnp.concatenate([jnp.repeat(mod[:bp], tp // CHUNK, axis=0),
                                jnp.repeat(mod[bp:], ts // CHUNK, axis=0)], axis=0)
        ng = lambda i: norm_g[l, i][None]

        h = _norm_mod(x, modg, ng(0), 0, 1)
        act = _gateup(h, w1_gate[l].astype(bf16), w1_up[l].astype(bf16))
        x = _mm_norm_res(act, w1_down[l].astype(bf16), x, modg, ng(1), 2, FFN_WEIGHT)

        h = _norm_mod(x, modg, ng(2), 3, 4)
        proj = _mm(h, _reorder_w_in(w_in[l]))
        kidx = _kidx(proj, idx_ln_g[l][None], idx_ln_b[l][None])

        kv_p = proj[:mp, COL_KV:COL_KV + 2 * KV_WIDTH].reshape(bp, tp, 2 * KV_WIDTH)
        k_p, v_p = kv_p[..., :KV_WIDTH], kv_p[..., KV_WIDTH:]
        ki_p = kidx[:mp].reshape(bp, tp, IDX_DIM)
        kv_s = proj[mp:, COL_KV:COL_KV + 2 * KV_WIDTH].reshape(bs, ts, 2 * KV_WIDTH)
        k_s, v_s = kv_s[..., :KV_WIDTH], kv_s[..., KV_WIDTH:]
        ki_s = kidx[mp:].reshape(bs, ts, IDX_DIM)

        def keys(k_all, v_all, ki_all, tk):
            n = k_all.shape[1]
            pad = ((0, 0), (0, -n % tk), (0, 0))
            kt = jnp.swapaxes(jnp.pad(k_all.astype(bf16), pad), 1, 2)
            kit = jnp.swapaxes(jnp.pad(ki_all.astype(bf16), pad), 1, 2)
            return kit, kt, jnp.pad(v_all.astype(bf16), pad), n

        kit, kt, vv, n_keys = keys(k_p, v_p, ki_p, DSA_TK_PROMPT)
        oa_p = _dsa(proj, 0, bp, tp, kit, kt, vv, tk=DSA_TK_PROMPT, past=0, n_keys=n_keys,
                    topk=min(TOPK_MAX, n_keys // 4))
        kit, kt, vv, n_keys = keys(
            jnp.concatenate([cache_k[l].reshape(bs, past, KV_WIDTH), k_s], axis=1),
            jnp.concatenate([cache_v[l].reshape(bs, past, KV_WIDTH), v_s], axis=1),
            jnp.concatenate([cache_kidx[l], ki_s], axis=1), DSA_TK_SAMPLE)
        oa_s = _dsa(proj, mp, bs, ts, kit, kt, vv, tk=DSA_TK_SAMPLE, past=past, n_keys=n_keys,
                    topk=min(TOPK_MAX, n_keys // 4))

        alog_row, dtb_row = _lane_row(a_log[l], SM_AB), _lane_row(dt_bias[l], SM_AB)
        gn = gdn_norm_g[l][None]
        ob_p, gdn_p = _gdn(proj, 0, bp, tp, conv_w[l], alog_row, dtb_row, gn,
                           jnp.zeros((bp, 8, CONV_CH), f32), jnp.zeros((bp, B_HEADS, B_KEY_DIM, B_VAL_DIM), f32))
        buf8 = jnp.pad(state_conv[l], ((0, 0), (8 - (CONV_WIDTH - 1), 0), (0, 0)))
        ob_s, gdn_s = _gdn(proj, mp, bs, ts, conv_w[l], alog_row, dtb_row, gn, buf8, state_gdn[l])

        merged = _merge(jnp.concatenate([oa_p, oa_s], axis=0), jnp.concatenate([ob_p, ob_s], axis=0),
                        w_branch_a[l].astype(bf16), w_branch_b[l].astype(bf16), proj)
        x = _mm_norm_res(merged, w_out[l].astype(bf16), x, modg, ng(3), 5, None)

        h = _norm_mod(x, modg, ng(4), 6, 7)
        act = _gateup(h, w2_gate[l].astype(bf16), w2_up[l].astype(bf16))
        x = _mm_norm_res(act, w2_down[l].astype(bf16), x, modg, ng(5), 8, FFN_WEIGHT)

        tail = CONV_WIDTH - 1
        conv_p = proj[:mp, :CONV_CH].reshape(bp, tp, CONV_CH)[:, tp - tail:]
        conv_s = proj[mp:, :CONV_CH].reshape(bs, ts, CONV_CH)[:, ts - tail:]
        outs_p.append((k_p.reshape(bp, tp, A_KV_HEADS, A_HEAD_DIM), v_p.reshape(bp, tp, A_KV_HEADS, A_HEAD_DIM),
                       ki_p, conv_p, gdn_p))
        outs_s.append((k_s.reshape(bs, ts, A_KV_HEADS, A_HEAD_DIM), v_s.reshape(bs, ts, A_KV_HEADS, A_HEAD_DIM),
                       ki_s, conv_s, gdn_s))

    stack = lambda outs: [jnp.stack([o[i] for o in outs], axis=0) for i in range(5)]
    return (x[:mp].reshape(bp, tp, d), x[mp:].reshape(bs, ts, d), *stack(outs_p), *stack(outs_s))
```

```python
import functools

import jax
import jax.numpy as jnp
from jax import lax
from jax.experimental import pallas as pl
from jax.experimental.pallas import tpu as pltpu

f32 = jnp.float32
bf16 = jnp.bfloat16
i32 = jnp.int32
i16 = jnp.int16

CHUNK = 64
A_HEADS, A_KV_HEADS, A_HEAD_DIM = 8, 2, 128
A_GROUP = A_HEADS // A_KV_HEADS
A_WIDTH = A_HEADS * A_HEAD_DIM
KV_WIDTH = A_KV_HEADS * A_HEAD_DIM
IDX_HEADS, IDX_DIM = 8, 64
TOPK_MAX = 256
B_HEADS, B_KEY_DIM, B_VAL_DIM = 8, 128, 128
B_QK_WIDTH = B_HEADS * B_KEY_DIM
B_WIDTH = B_HEADS * B_VAL_DIM
CONV_WIDTH = 4
CONV_CH = 2 * B_QK_WIDTH + B_WIDTH
N_MOD = 9
FFN_WEIGHT = 0.5
EPS = 1e-6

COL_QKVB, COL_QA, COL_GA, COL_GB, COL_ZB, COL_KV, COL_QI, COL_SM = 0, 3072, 4096, 6144, 8192, 9216, 9728, 10240
SM_WIDTH = 128
SM_WI, SM_AB, SM_BB = 64, 72, 80
PROJ_COLS = COL_SM + SM_WIDTH

INT_MIN = -(2 ** 31)
INT_MAX = 2 ** 31 - 1
I16_MIN, I16_MAX = -(2 ** 15), 2 ** 15 - 1
NEG = -1e30
MASKED_BF16_BITS = -3638
LANES = 128
MXU_DEPTH = 256
LOG2E = 1.4426950408889634
DSA_TK = 640
ATT_ROWS = 64
VMEM_LIMIT = 56 * 1024 * 1024


def _pick(n, prefs):
    for p in prefs:
        if n % p == 0:
            return p
    raise ValueError(f"no tile of {prefs} divides {n}")


def _params(*sem):
    return pltpu.CompilerParams(dimension_semantics=sem, vmem_limit_bytes=VMEM_LIMIT)


def _rms(y, g):
    return y * lax.rsqrt(jnp.mean(y * y, axis=-1, keepdims=True) + EPS) * g


def _ada_kernel(c_ref, w_ref, b_ref, o_ref):
    o_ref[...] = jnp.dot(c_ref[...].astype(bf16), w_ref[...].astype(bf16),
                         preferred_element_type=f32) + b_ref[...]


def _ada(c, w, b):
    m, d = c.shape
    n = w.shape[1]
    tn = _pick(n, (1024, 512, 256, 128))
    return pl.pallas_call(
        _ada_kernel,
        grid=(n // tn,),
        in_specs=[pl.BlockSpec((m, d), lambda j: (0, 0)),
                  pl.BlockSpec((d, tn), lambda j: (0, j)),
                  pl.BlockSpec((1, tn), lambda j: (0, j))],
        out_specs=pl.BlockSpec((m, tn), lambda j: (0, j)),
        out_shape=jax.ShapeDtypeStruct((m, n), f32),
        compiler_params=_params("arbitrary"),
        name="ada",
    )(c, w, b)


def _norm_mod_kernel(x_ref, mod_ref, g_ref, o_ref, *, sh, sc, nb):
    g = g_ref[...]
    for j in range(nb):
        rows = slice(j * CHUNK, (j + 1) * CHUNK)
        y = _rms(x_ref[rows, :], g)
        o_ref[rows, :] = (y * (1.0 + mod_ref[j, sc:sc + 1, :]) + mod_ref[j, sh:sh + 1, :]).astype(bf16)


def _norm_mod(x, modg, g, sh, sc):
    m, d = x.shape
    tm = _pick(m, (512, 256, 128, 64))
    nb = tm // CHUNK
    return pl.pallas_call(
        functools.partial(_norm_mod_kernel, sh=sh, sc=sc, nb=nb),
        grid=(m // tm,),
        in_specs=[pl.BlockSpec((tm, d), lambda i: (i, 0)),
                  pl.BlockSpec((nb, N_MOD, d), lambda i: (i, 0, 0)),
                  pl.BlockSpec((1, d), lambda i: (0, 0))],
        out_specs=pl.BlockSpec((tm, d), lambda i: (i, 0)),
        out_shape=jax.ShapeDtypeStruct((m, d), bf16),
        compiler_params=_params("arbitrary"),
        name="norm_mod",
    )(x, modg, g)


def _gateup_kernel(h_ref, wg_ref, wu_ref, o_ref):
    h = h_ref[...]
    a = jnp.dot(h, wg_ref[...], preferred_element_type=f32)
    u = jnp.dot(h, wu_ref[...], preferred_element_type=f32)
    o_ref[...] = (a * jax.nn.sigmoid(a) * u).astype(bf16)


def _gateup(h, wg, wu):
    m, d = h.shape
    n = wg.shape[1]
    tm = _pick(m, (1024, 512, 256, 128, 64))
    tn = _pick(n, (512, 256, 128))
    return pl.pallas_call(
        _gateup_kernel,
        grid=(m // tm, n // tn),
        in_specs=[pl.BlockSpec((tm, d), lambda i, j: (i, 0)),
                  pl.BlockSpec((d, tn), lambda i, j: (0, j)),
                  pl.BlockSpec((d, tn), lambda i, j: (0, j))],
        out_specs=pl.BlockSpec((tm, tn), lambda i, j: (i, j)),
        out_shape=jax.ShapeDtypeStruct((m, n), bf16),
        compiler_params=_params("arbitrary", "arbitrary"),
        name="gateup",
    )(h, wg, wu)


def _mm_norm_res_kernel(a_ref, w_ref, x_ref, mod_ref, g_ref, o_ref, y_scr, *, gt, coeff, nb, nj, tn):
    j = pl.program_id(1)
    y_scr[:, pl.ds(pl.multiple_of(j * tn, tn), tn)] = jnp.dot(a_ref[...], w_ref[...], preferred_element_type=f32)

    @pl.when(j == nj - 1)
    def _():
        g = g_ref[...]
        for jj in range(nb):
            rows = slice(jj * CHUNK, (jj + 1) * CHUNK)
            gate = mod_ref[jj, gt:gt + 1, :]
            if coeff is not None:
                gate = coeff * gate
            o_ref[rows, :] = x_ref[rows, :] + gate * _rms(y_scr[rows, :], g)


def _mm_norm_res(a, w, x, modg, g, gt, coeff):
    m, kd = a.shape
    d = w.shape[1]
    tm = _pick(m, (512, 256, 128, 64))
    tn = _pick(d, (512, 256, 128))
    nb, nj = tm // CHUNK, d // tn
    return pl.pallas_call(
        functools.partial(_mm_norm_res_kernel, gt=gt, coeff=coeff, nb=nb, nj=nj, tn=tn),
        grid=(m // tm, nj),
        in_specs=[pl.BlockSpec((tm, kd), lambda i, j: (i, 0)),
                  pl.BlockSpec((kd, tn), lambda i, j: (0, j)),
                  pl.BlockSpec((tm, d), lambda i, j: (i, 0)),
                  pl.BlockSpec((nb, N_MOD, d), lambda i, j: (i, 0, 0)),
                  pl.BlockSpec((1, d), lambda i, j: (0, 0))],
        out_specs=pl.BlockSpec((tm, d), lambda i, j: (i, 0)),
        out_shape=jax.ShapeDtypeStruct((m, d), f32),
        scratch_shapes=[pltpu.VMEM((tm, d), f32)],
        compiler_params=_params("arbitrary", "arbitrary"),
        name="mm_norm_res",
    )(a, w, x, modg, g)


def _mm_kernel(a_ref, w_ref, o_ref):
    o_ref[...] = jnp.dot(a_ref[...], w_ref[...], preferred_element_type=f32)


def _mm(a, w):
    m, kd = a.shape
    n = w.shape[1]
    tm = _pick(m, (1024, 512, 256, 128, 64))
    tn = _pick(n, (1152, 1024, 512, 384, 256, 128))
    return pl.pallas_call(
        _mm_kernel,
        grid=(m // tm, n // tn),
        in_specs=[pl.BlockSpec((tm, kd), lambda i, j: (i, 0)),
                  pl.BlockSpec((kd, tn), lambda i, j: (0, j))],
        out_specs=pl.BlockSpec((tm, tn), lambda i, j: (i, j)),
        out_shape=jax.ShapeDtypeStruct((m, n), f32),
        compiler_params=_params("arbitrary", "arbitrary"),
        name="in_proj",
    )(a, w)


def _kidx_kernel(sm_ref, g_ref, b_ref, o_ref):
    x = sm_ref[:, 0:IDX_DIM]
    xc = x - jnp.mean(x, axis=-1, keepdims=True)
    y = xc * lax.rsqrt(jnp.mean(xc * xc, axis=-1, keepdims=True) + EPS)
    o_ref[...] = y * g_ref[...] + b_ref[...]


def _kidx(proj, g, b):
    m = proj.shape[0]
    tm = _pick(m, (1024, 512, 256, 128, 64))
    return pl.pallas_call(
        _kidx_kernel,
        grid=(m // tm,),
        in_specs=[pl.BlockSpec((tm, SM_WIDTH), lambda i: (i, COL_SM // SM_WIDTH)),
                  pl.BlockSpec((1, IDX_DIM), lambda i: (0, 0)),
                  pl.BlockSpec((1, IDX_DIM), lambda i: (0, 0))],
        out_specs=pl.BlockSpec((tm, IDX_DIM), lambda i: (i, 0)),
        out_shape=jax.ShapeDtypeStruct((m, IDX_DIM), f32),
        compiler_params=_params("arbitrary"),
        name="kidx_ln",
    )(proj, g, b)


def _dsa_kernel(qi_ref, sm_ref, qa_ref, kit_ref, kt_ref, v_ref, o_ref,
                hi_ref, lo_ref, w_scr, q_scr, p_scr, m_scr, acc_scr, *, tq, tk, past, n_keys, topk, nbits):
    chunk_shift = CHUNK.bit_length() - 1
    nlc = tk // LANES
    qpos0 = past + pl.program_id(1) * tq
    limit = jnp.minimum((((qpos0 + tq - 1) >> chunk_shift) + 1) * CHUNK, n_keys)
    nkb = (limit + tk - 1) // tk
    row = lax.broadcasted_iota(i32, (tq, LANES), 0)
    lane = lax.broadcasted_iota(i32, (tq, LANES), 1)
    qchunk = (qpos0 + row) >> chunk_shift

    def lane_chunks(kb):
        c0 = pl.multiple_of(kb * tk, tk)
        return c0, [pl.multiple_of(c0 + c * LANES, LANES) for c in range(nlc)]

    qi = qi_ref[...].astype(bf16)
    qis = [qi[:, h * IDX_DIM:(h + 1) * IDX_DIM] for h in range(IDX_HEADS)]
    w = sm_ref[:, SM_WI:SM_WI + IDX_HEADS] * (IDX_HEADS ** -0.5 * IDX_DIM ** -0.5)
    for h in range(IDX_HEADS):
        w_scr[h] = jnp.broadcast_to(w[:, h:h + 1], (tq, LANES))

    def score_body(masked, kb, carry):
        c0, ccs = lane_chunks(kb)
        kit = kit_ref[:, pl.ds(c0, tk)]
        sc = [jnp.zeros((tq, LANES), f32)] * nlc
        for h in range(IDX_HEADS):
            s = jnp.dot(qis[h], kit, preferred_element_type=f32)
            wh = w_scr[h]
            sc = [sc[c] + wh * jnp.maximum(s[:, c * LANES:(c + 1) * LANES], 0.0) for c in range(nlc)]
        for c, cc in enumerate(ccs):
            bits = lax.bitcast_convert_type(sc[c], i32)
            key = jnp.where(bits < 0, INT_MIN - bits, bits)
            if masked:
                kpos = cc + lane
                allowed = ((kpos >> chunk_shift) <= qchunk) & (kpos < n_keys)
                key = jnp.where(allowed, key, INT_MIN)
            hi_ref[:, pl.ds(cc, LANES)] = (key >> 16).astype(i16)
            lo_ref[:, pl.ds(cc, LANES)] = (key ^ 0x8000).astype(i16)
        return carry

    nfull = jnp.minimum((qpos0 >> chunk_shift) * CHUNK, n_keys) // tk
    lax.fori_loop(0, nfull, functools.partial(score_body, False), 0)
    lax.fori_loop(nfull, nkb, functools.partial(score_body, True), 0)

    def count16(pred):
        def body(kb, acc):
            _, ccs = lane_chunks(kb)
            for cc in ccs:
                acc = acc + jnp.where(pred(cc), jnp.int16(1), jnp.int16(0))
            return acc
        acc = lax.fori_loop(0, nkb, body, jnp.zeros((tq, LANES), i16))
        return jnp.broadcast_to(jnp.sum(acc.astype(f32), axis=1, keepdims=True), (tq, LANES))

    hi_at = lambda cc: hi_ref[:, pl.ds(cc, LANES)]
    lo_at = lambda cc: lo_ref[:, pl.ds(cc, LANES)]
    kpos16 = lambda cc: (cc + lane).astype(i16)
    kf = float(topk)

    def bisect16(at, ct0):
        def bit_body(it, carry):
            t, ct = carry
            cand = t + lax.shift_left(jnp.int32(1), 15 - it)
            c16 = cand.astype(i16)
            cnt = count16(lambda cc: at(cc) >= c16)
            take = cnt >= kf
            return jnp.where(take, cand, t), jnp.where(take, cnt, ct)
        return lax.fori_loop(0, 16, bit_body, (jnp.full((tq, LANES), I16_MIN, i32), ct0))

    th, ct = bisect16(hi_at, jnp.zeros((tq, LANES), f32) + (nkb * tk).astype(f32))
    th16 = th.astype(i16)

    def mask_body(kb, carry):
        _, ccs = lane_chunks(kb)
        for cc in ccs:
            hi = hi_at(cc)
            lo_ref[:, pl.ds(cc, LANES)] = jnp.where(hi > th16, jnp.int16(I16_MAX),
                                                    jnp.where(hi == th16, lo_at(cc), jnp.int16(I16_MIN)))
        return carry

    lax.fori_loop(0, nkb, mask_body, 0)
    tl, ct = bisect16(lo_at, ct)
    tl16 = tl.astype(i16)

    def tie_cut():
        r = kf - count16(lambda cc: (hi_at(cc) > th16) | ((hi_at(cc) == th16) & (lo_at(cc) > tl16)))

        def j_body(it, j):
            cand = j + lax.shift_left(jnp.int32(1), nbits - 1 - it)
            c16 = cand.astype(i16)
            cnt = count16(lambda cc: (hi_at(cc) == th16) & (lo_at(cc) == tl16) & (kpos16(cc) < c16))
            return jnp.where(cnt < r, cand, j)

        return lax.fori_loop(0, nbits, j_body, jnp.zeros((tq, LANES), i32))

    jcut = lax.cond(jnp.max(ct) > kf, tie_cut, lambda: jnp.full((tq, LANES), I16_MAX, i32)).astype(i16)
    tls16 = jnp.where(th == I16_MIN, jnp.maximum(tl, I16_MIN + 1), tl).astype(i16)

    def bias_body(kb, carry):
        _, ccs = lane_chunks(kb)
        for cc in ccs:
            hi, lo = hi_at(cc), lo_at(cc)
            sel = (hi > th16) | ((hi == th16) & ((lo > tls16) | ((lo == tls16) & (kpos16(cc) <= jcut))))
            hi_ref[:, pl.ds(cc, LANES)] = jnp.where(sel, jnp.int16(0), jnp.int16(MASKED_BF16_BITS))
        return carry

    lax.fori_loop(0, nkb, bias_body, 0)

    qa = qa_ref[...] * (A_HEAD_DIM ** -0.5 * LOG2E)
    for n in range(A_KV_HEADS):
        q_scr[n] = jnp.concatenate(
            [qa[:, (n * A_GROUP + g) * A_HEAD_DIM:(n * A_GROUP + g + 1) * A_HEAD_DIM] for g in range(A_GROUP)],
            axis=0).astype(bf16)
    m_scr[...] = jnp.full(m_scr.shape, NEG, f32)
    acc_scr[...] = jnp.zeros(acc_scr.shape, f32)

    def att_body(kb, carry):
        c0, ccs = lane_chunks(kb)
        bias = [lax.bitcast_convert_type(hi_at(cc), bf16).astype(f32) for cc in ccs]
        heads = range(A_KV_HEADS)
        blocks = [(slice(blk * ATT_ROWS, (blk + 1) * ATT_ROWS),
                   slice(blk * ATT_ROWS % tq, blk * ATT_ROWS % tq + ATT_ROWS))
                  for blk in range(A_GROUP * tq // ATT_ROWS)]
        s_all = [jnp.dot(q_scr[n], kt_ref[n * A_HEAD_DIM:(n + 1) * A_HEAD_DIM, pl.ds(c0, tk)],
                         preferred_element_type=f32) for n in heads]
        m_news, alphas = [[] for _ in heads], [[] for _ in heads]
        for n in heads:
            for rows, qrows in blocks:
                mx = s_all[n][rows, 0:LANES] + bias[0][qrows]
                for c in range(1, nlc):
                    mx = jnp.maximum(mx, s_all[n][rows, c * LANES:(c + 1) * LANES] + bias[c][qrows])
                m_old = m_scr[n, rows]
                m_new = jnp.maximum(m_old, jnp.max(mx, axis=1, keepdims=True))
                m_scr[n, rows] = m_new
                m_news[n].append(m_new)
                alphas[n].append(jnp.exp2(m_old - m_new))
        for n in heads:
            for blk, (rows, qrows) in enumerate(blocks):
                for c in range(nlc):
                    p = jnp.exp2(s_all[n][rows, c * LANES:(c + 1) * LANES] - (m_news[n][blk] - bias[c][qrows]))
                    p_scr[n, rows, c * LANES:(c + 1) * LANES] = p.astype(bf16)
        for n in heads:
            vv = v_ref[pl.ds(c0, tk), n * 2 * A_HEAD_DIM:(n + 1) * 2 * A_HEAD_DIM]
            alpha = jnp.concatenate(alphas[n], axis=0)
            acc_scr[n] = (jnp.concatenate([alpha, alpha], axis=1) * acc_scr[n]
                          + jnp.dot(p_scr[n], vv, preferred_element_type=f32))
        return carry

    lax.fori_loop(0, nkb, att_body, 0)

    for n in range(A_KV_HEADS):
        o = acc_scr[n, :, 0:A_HEAD_DIM] / acc_scr[n, :, A_HEAD_DIM:2 * A_HEAD_DIM]
        for g in range(A_GROUP):
            hh = n * A_GROUP + g
            o_ref[:, hh * A_HEAD_DIM:(hh + 1) * A_HEAD_DIM] = o[g * tq:(g + 1) * tq].astype(bf16)


def _dsa(proj, row0, nbatch, t_len, kit, kt, v, *, tk, past, n_keys, topk):
    lp = v.shape[1]
    tq = _pick(t_len, (128, 64))
    assert lp % tk == 0 and tk % LANES == 0 and row0 % tq == 0 and past % CHUNK == 0
    assert lp <= I16_MAX and tq % ATT_ROWS == 0
    nq = t_len // tq
    rb = row0 // tq
    rows = A_GROUP * tq

    def qmap(width_col):
        return lambda b, i: (rb + b * nq + i, width_col)

    return pl.pallas_call(
        functools.partial(_dsa_kernel, tq=tq, tk=tk, past=past, n_keys=n_keys, topk=topk,
                          nbits=lp.bit_length()),
        grid=(nbatch, nq),
        in_specs=[pl.BlockSpec((tq, IDX_HEADS * IDX_DIM), qmap(COL_QI // (IDX_HEADS * IDX_DIM))),
                  pl.BlockSpec((tq, SM_WIDTH), qmap(COL_SM // SM_WIDTH)),
                  pl.BlockSpec((tq, A_WIDTH), qmap(COL_QA // A_WIDTH)),
                  pl.BlockSpec((None, IDX_DIM, lp), lambda b, i: (b, 0, 0), pipeline_mode=pl.Buffered(1)),
                  pl.BlockSpec((None, KV_WIDTH, lp), lambda b, i: (b, 0, 0), pipeline_mode=pl.Buffered(1)),
                  pl.BlockSpec((None, lp, 2 * KV_WIDTH), lambda b, i: (b, 0, 0), pipeline_mode=pl.Buffered(1))],
        out_specs=pl.BlockSpec((tq, A_WIDTH), lambda b, i: (b * nq + i, 0)),
        out_shape=jax.ShapeDtypeStruct((nbatch * t_len, A_WIDTH), bf16),
        scratch_shapes=[pltpu.VMEM((tq, lp), i16),
                        pltpu.VMEM((tq, lp), i16),
                        pltpu.VMEM((IDX_HEADS, tq, LANES), f32),
                        pltpu.VMEM((A_KV_HEADS, rows, A_HEAD_DIM), bf16),
                        pltpu.VMEM((A_KV_HEADS, rows, tk), bf16),
                        pltpu.VMEM((A_KV_HEADS, rows, LANES), f32),
                        pltpu.VMEM((A_KV_HEADS, rows, 2 * A_HEAD_DIM), f32)],
        compiler_params=_params("arbitrary", "arbitrary"),
        name="dsa",
    )(proj, proj, proj, kit, kt, v)


_NN = (((1,), (0,)), ((), ()))
_NT = (((1,), (1,)), ((), ()))
_TN = (((0,), (0,)), ((), ()))


def _split(a):
    hi = a.astype(bf16)
    return hi, (a - hi.astype(f32)).astype(bf16)


def _dot1(a, b, dims=_NN):
    return lax.dot_general(a.astype(bf16), b.astype(bf16), dims, preferred_element_type=f32)


def _dot3_sq(a, b):
    ah, al = _split(a)
    bh, bl = _split(b)
    pad = MXU_DEPTH - 3 * CHUNK
    lhs = jnp.concatenate([ah, ah, al, jnp.zeros((CHUNK, pad), bf16)], axis=1)
    rhs = jnp.concatenate([bh, bl, bh, jnp.zeros((pad, CHUNK), bf16)], axis=0)
    return jnp.dot(lhs, rhs, preferred_element_type=f32)


def _gdn_kernel(qkv_ref, sm_ref, z_ref, cw_ref, alog_ref, dtb_ref, gn_ref, buf_ref, s0_ref,
                ob_ref, sout_ref, xp_scr, s_scr, *, nc):
    c = pl.program_id(1)
    chunk_shift = CHUNK.bit_length() - 1

    @pl.when(c == 0)
    def _():
        xp_scr[0:8, :] = buf_ref[...]
        s_scr[...] = s0_ref[...]

    x = qkv_ref[...]
    xp_scr[8:8 + CHUNK, :] = x
    cw = cw_ref[...]
    y = (cw[0:1] * xp_scr[5:5 + CHUNK, :] + cw[1:2] * xp_scr[6:6 + CHUNK, :]
         + cw[2:3] * xp_scr[7:7 + CHUNK, :] + cw[3:4] * x)
    xp_scr[0:8, :] = x[CHUNK - 8:CHUNK, :]
    y = y * jax.nn.sigmoid(y)

    sm = sm_ref[...]
    z = sm + dtb_ref[...]
    softplus = jnp.maximum(z, 0.0) + jnp.log1p(jnp.exp(-jnp.abs(z)))
    g_all = -jnp.exp(alog_ref[...]) * softplus
    beta_all = jax.nn.sigmoid(sm)
    r = lax.broadcasted_iota(i32, (CHUNK, CHUNK), 0)
    cc = lax.broadcasted_iota(i32, (CHUNK, CHUNK), 1)
    tri, strict = r >= cc, r > cc
    eye = jnp.where(r == cc, 1.0, 0.0)
    gc = lax.dot_general(jnp.where(tri, 1.0, 0.0), g_all, _NN,
                         precision=lax.Precision.HIGHEST, preferred_element_type=f32)
    gc_t = gc.T
    g_last = gc[CHUNK - 1:CHUNK, :]
    e_gc, e_last, e_rest = jnp.exp(gc), jnp.exp(g_last), jnp.exp(g_last - gc)
    gn = gn_ref[...]

    heads = range(B_HEADS)
    hs = [slice(h * B_KEY_DIM, (h + 1) * B_KEY_DIM) for h in heads]
    col = lambda a, h: a[:, SM_AB + h:SM_AB + h + 1]
    qs, ks, vbs, kbs, decays = [], [], [], [], []
    for h in heads:
        q = y[:, hs[h]]
        k = y[:, B_QK_WIDTH + h * B_KEY_DIM:B_QK_WIDTH + (h + 1) * B_KEY_DIM]
        v = y[:, 2 * B_QK_WIDTH + h * B_VAL_DIM:2 * B_QK_WIDTH + (h + 1) * B_VAL_DIM]
        beta = beta_all[:, SM_BB + h:SM_BB + h + 1]
        qs.append(q * lax.rsqrt(jnp.sum(q * q, axis=-1, keepdims=True) + EPS) * (B_KEY_DIM ** -0.5))
        k = k * lax.rsqrt(jnp.sum(k * k, axis=-1, keepdims=True) + EPS)
        ks.append(k)
        kbs.append(k * beta)
        vbs.append(v * beta)
        diff = col(gc, h) - gc_t[SM_AB + h:SM_AB + h + 1, :]
        decays.append(jnp.where(tri, jnp.exp(jnp.where(tri, diff, 0.0)), 0.0))
    lows = [jnp.where(strict, _dot1(kbs[h], ks[h], _NT) * decays[h], 0.0) for h in heads]
    xpows, ainvs = lows, [eye - low for low in lows]
    for _ in range(chunk_shift - 1):
        xpows = [_dot3_sq(xp, xp) for xp in xpows]
        ainvs = [ainvs[h] + _dot3_sq(ainvs[h], xpows[h]) for h in heads]
    sols = [_dot1(ainvs[h], jnp.concatenate([vbs[h], kbs[h] * col(e_gc, h)], axis=1)) for h in heads]
    qks = [_dot1(qs[h], ks[h], _NT) * decays[h] for h in heads]
    states = [s_scr[h] for h in heads]
    v_news = [sols[h][:, :B_VAL_DIM] - _dot1(sols[h][:, B_VAL_DIM:], states[h]) for h in heads]
    outs = [_dot1(qs[h] * col(e_gc, h), states[h]) + _dot1(qks[h], v_news[h]) for h in heads]
    for h in heads:
        s_scr[h] = states[h] * col(e_last, h) + _dot1(ks[h] * col(e_rest, h), v_news[h], _TN)
    for h in heads:
        zh = z_ref[:, hs[h]]
        ob_ref[:, hs[h]] = (_rms(outs[h], gn) * (zh * jax.nn.sigmoid(zh))).astype(bf16)

    @pl.when(c == nc - 1)
    def _():
        sout_ref[...] = s_scr[...]


def _gdn(proj, row0, nbatch, t_len, conv_w, alog_row, dtb_row, gn, buf8, s0):
    assert t_len % CHUNK == 0 and row0 % CHUNK == 0
    nc = t_len // CHUNK
    rb = row0 // CHUNK

    def rmap(col):
        return lambda b, c: (rb + b * nc + c, col)

    return pl.pallas_call(
        functools.partial(_gdn_kernel, nc=nc),
        grid=(nbatch, nc),
        in_specs=[pl.BlockSpec((CHUNK, CONV_CH), rmap(COL_QKVB // CONV_CH)),
                  pl.BlockSpec((CHUNK, SM_WIDTH), rmap(COL_SM // SM_WIDTH)),
                  pl.BlockSpec((CHUNK, B_WIDTH), rmap(COL_ZB // B_WIDTH)),
                  pl.BlockSpec((CONV_WIDTH, CONV_CH), lambda b, c: (0, 0)),
                  pl.BlockSpec((1, SM_WIDTH), lambda b, c: (0, 0)),
                  pl.BlockSpec((1, SM_WIDTH), lambda b, c: (0, 0)),
                  pl.BlockSpec((1, B_VAL_DIM), lambda b, c: (0, 0)),
                  pl.BlockSpec((None, 8, CONV_CH), lambda b, c: (b, 0, 0)),
                  pl.BlockSpec((None, B_HEADS, B_KEY_DIM, B_VAL_DIM), lambda b, c: (b, 0, 0, 0))],
        out_specs=[pl.BlockSpec((CHUNK, B_WIDTH), lambda b, c: (b * nc + c, 0)),
                   pl.BlockSpec((None, B_HEADS, B_KEY_DIM, B_VAL_DIM), lambda b, c: (b, 0, 0, 0))],
        out_shape=[jax.ShapeDtypeStruct((nbatch * t_len, B_WIDTH), bf16),
                   jax.ShapeDtypeStruct((nbatch, B_HEADS, B_KEY_DIM, B_VAL_DIM), f32)],
        scratch_shapes=[pltpu.VMEM((8 + CHUNK, CONV_CH), f32),
                        pltpu.VMEM((B_HEADS, B_KEY_DIM, B_VAL_DIM), f32)],
        compiler_params=_params("arbitrary", "arbitrary"),
        name="gdn",
    )(proj, proj, proj, conv_w, alog_row, dtb_row, gn, buf8, s0)


def _merge_kernel(oa_ref, ob_ref, wa_ref, wb_ref, ga_ref, gb_ref, o_ref):
    a = jnp.dot(oa_ref[...], wa_ref[...], preferred_element_type=f32)
    b = jnp.dot(ob_ref[...], wb_ref[...], preferred_element_type=f32)
    o_ref[...] = (jax.nn.sigmoid(ga_ref[...]) * a + jax.nn.sigmoid(gb_ref[...]) * b).astype(bf16)


def _merge(oa, ob, wa, wb, proj):
    m = oa.shape[0]
    d = wa.shape[1]
    tm = _pick(m, (1024, 512, 256, 128, 64))
    tn = 512
    assert d % tn == 0 and COL_GA % tn == 0 and COL_GB % tn == 0
    return pl.pallas_call(
        _merge_kernel,
        grid=(m // tm, d // tn),
        in_specs=[pl.BlockSpec((tm, A_WIDTH), lambda i, j: (i, 0)),
                  pl.BlockSpec((tm, B_WIDTH), lambda i, j: (i, 0)),
                  pl.BlockSpec((A_WIDTH, tn), lambda i, j: (0, j)),
                  pl.BlockSpec((B_WIDTH, tn), lambda i, j: (0, j)),
                  pl.BlockSpec((tm, tn), lambda i, j: (i, COL_GA // tn + j)),
                  pl.BlockSpec((tm, tn), lambda i, j: (i, COL_GB // tn + j))],
        out_specs=pl.BlockSpec((tm, tn), lambda i, j: (i, j)),
        out_shape=jax.ShapeDtypeStruct((m, d), bf16),
        compiler_params=_params("arbitrary", "arbitrary"),
        name="merge",
    )(oa, ob, wa, wb, proj, proj)


def _reorder_w_in(w):
    o, pts = 0, {}
    for name, width in (("qa", A_WIDTH), ("ka", KV_WIDTH), ("va", KV_WIDTH), ("qi", IDX_HEADS * IDX_DIM),
                        ("ki", IDX_DIM), ("wi", IDX_HEADS), ("qb", B_QK_WIDTH), ("kb", B_QK_WIDTH),
                        ("vb", B_WIDTH), ("zb", B_WIDTH), ("ab", B_HEADS), ("bb", B_HEADS)):
        pts[name] = (o, o + width)
        o += width
    d = w.shape[0]
    rest = w.shape[1] - o
    assert rest == 2 * d
    seg = lambda a, b: w[:, pts[a][0]:pts[b][1]]
    used = IDX_DIM + IDX_HEADS + 2 * B_HEADS
    out = jnp.concatenate([seg("qb", "vb"), seg("qa", "qa"), w[:, o:o + d], w[:, o + d:o + 2 * d], seg("zb", "zb"),
                           seg("ka", "va"), seg("qi", "qi"), seg("ki", "wi"), seg("ab", "bb"),
                           jnp.zeros((d, SM_WIDTH - used), w.dtype)], axis=1)
    assert out.shape[1] == PROJ_COLS and COL_GA == COL_QA + A_WIDTH and COL_GB == COL_GA + d
    return out.astype(bf16)


def _lane_row(vals, start):
    return jnp.zeros((1, SM_WIDTH), f32).at[0, start:start + vals.shape[0]].set(vals.astype(f32))


def kernel(x_prompt, x_sample, cache_k, cache_v, cache_kidx, state_conv, state_gdn, c_prompt, c_sample, w_ada, b_ada, norm_g, w1_gate, w1_up, w1_down, w_in, idx_ln_g, idx_ln_b, conv_w, a_log, dt_bias, gdn_norm_g, w_branch_a, w_branch_b, w_out, w2_gate, w2_up, w2_down):
    bp, tp, d = x_prompt.shape
    bs, ts, _ = x_sample.shape
    depth = w_ada.shape[0]
    past = cache_k.shape[2]
    assert tp % CHUNK == 0 and ts % CHUNK == 0 and ts >= CONV_WIDTH - 1 and tp >= CONV_WIDTH - 1
    mp, ms = bp * tp, bs * ts

    x = jnp.concatenate([x_prompt.reshape(mp, d), x_sample.reshape(ms, d)], axis=0)
    c_all = jnp.concatenate([c_prompt, c_sample], axis=0)
    c_pad = jnp.pad(c_all, ((0, -c_all.shape[0] % 8), (0, 0)))

    outs_p, outs_s = [], []
    for l in range(depth):
        mod = _ada(c_pad, w_ada[l], b_ada[l][None])[:bp + bs].reshape(bp + bs, N_MOD, d)
        modg = jnp.concatenate([jnp.repeat(mod[:bp], tp // CHUNK, axis=0),
                                jnp.repeat(mod[bp:], ts // CHUNK, axis=0)], axis=0)
        ng = lambda i: norm_g[l, i][None]

        h = _norm_mod(x, modg, ng(0), 0, 1)
        act = _gateup(h, w1_gate[l].astype(bf16), w1_up[l].astype(bf16))
        x = _mm_norm_res(act, w1_down[l].astype(bf16), x, modg, ng(1), 2, FFN_WEIGHT)

        h = _norm_mod(x, modg, ng(2), 3, 4)
        proj = _mm(h, _reorder_w_in(w_in[l]))
        kidx = _kidx(proj, idx_ln_g[l][None], idx_ln_b[l][None])

        kv_p = proj[:mp, COL_KV:COL_KV + 2 * KV_WIDTH].reshape(bp, tp, 2 * KV_WIDTH)
        k_p, v_p = kv_p[..., :KV_WIDTH], kv_p[..., KV_WIDTH:]
        ki_p = kidx[:mp].reshape(bp, tp, IDX_DIM)
        kv_s = proj[mp:, COL_KV:COL_KV + 2 * KV_WIDTH].reshape(bs, ts, 2 * KV_WIDTH)
        k_s, v_s = kv_s[..., :KV_WIDTH], kv_s[..., KV_WIDTH:]
        ki_s = kidx[mp:].reshape(bs, ts, IDX_DIM)

        def keys(k_all, v_all, ki_all):
            n = k_all.shape[1]
            lp = -(-n // DSA_TK) * DSA_TK
            lp += DSA_TK * (1 - lp // LANES % 2)
            pad = ((0, 0), (0, lp - n), (0, 0))
            kt = jnp.swapaxes(jnp.pad(k_all.astype(bf16), pad), 1, 2)
            kit = jnp.swapaxes(jnp.pad(ki_all.astype(bf16), pad), 1, 2)
            v4 = jnp.pad(v_all.astype(bf16), pad).reshape(v_all.shape[0], lp, A_KV_HEADS, A_HEAD_DIM)
            v_ext = jnp.concatenate([v4, jnp.ones_like(v4)], axis=-1).reshape(v_all.shape[0], lp, 2 * KV_WIDTH)
            return kit, kt, v_ext, n

        kit, kt, vv, n_keys = keys(k_p, v_p, ki_p)
        oa_p = _dsa(proj, 0, bp, tp, kit, kt, vv, tk=DSA_TK, past=0, n_keys=n_keys,
                    topk=min(TOPK_MAX, n_keys // 4))
        kit, kt, vv, n_keys = keys(
            jnp.concatenate([cache_k[l].reshape(bs, past, KV_WIDTH), k_s], axis=1),
            jnp.concatenate([cache_v[l].reshape(bs, past, KV_WIDTH), v_s], axis=1),
            jnp.concatenate([cache_kidx[l], ki_s], axis=1))
        oa_s = _dsa(proj, mp, bs, ts, kit, kt, vv, tk=DSA_TK, past=past, n_keys=n_keys,
                    topk=min(TOPK_MAX, n_keys // 4))

        alog_row, dtb_row = _lane_row(a_log[l], SM_AB), _lane_row(dt_bias[l], SM_AB)
        gn = gdn_norm_g[l][None]
        ob_p, gdn_p = _gdn(proj, 0, bp, tp, conv_w[l], alog_row, dtb_row, gn,
                           jnp.zeros((bp, 8, CONV_CH), f32), jnp.zeros((bp, B_HEADS, B_KEY_DIM, B_VAL_DIM), f32))
        buf8 = jnp.pad(state_conv[l], ((0, 0), (8 - (CONV_WIDTH - 1), 0), (0, 0)))
        ob_s, gdn_s = _gdn(proj, mp, bs, ts, conv_w[l], alog_row, dtb_row, gn, buf8, state_gdn[l])

        merged = _merge(jnp.concatenate([oa_p, oa_s], axis=0), jnp.concatenate([ob_p, ob_s], axis=0),
                        w_branch_a[l].astype(bf16), w_branch_b[l].astype(bf16), proj)
        x = _mm_norm_res(merged, w_out[l].astype(bf16), x, modg, ng(3), 5, None)

        h = _norm_mod(x, modg, ng(4), 6, 7)
        act = _gateup(h, w2_gate[l].astype(bf16), w2_up[l].astype(bf16))
        x = _mm_norm_res(act, w2_down[l].astype(bf16), x, modg, ng(5), 8, FFN_WEIGHT)

        tail = CONV_WIDTH - 1
        conv_p = proj[:mp, :CONV_CH].reshape(bp, tp, CONV_CH)[:, tp - tail:]
        conv_s = proj[mp:, :CONV_CH].reshape(bs, ts, CONV_CH)[:, ts - tail:]
        outs_p.append((k_p.reshape(bp, tp, A_KV_HEADS, A_HEAD_DIM), v_p.reshape(bp, tp, A_KV_HEADS, A_HEAD_DIM),
                       ki_p, conv_p, gdn_p))
        outs_s.append((k_s.reshape(bs, ts, A_KV_HEADS, A_HEAD_DIM), v_s.reshape(bs, ts, A_KV_HEADS, A_HEAD_DIM),
                       ki_s, conv_s, gdn_s))

    stack = lambda outs: [jnp.stack([o[i] for o in outs], axis=0) for i in range(5)]
    return (x[:mp].reshape(bp, tp, d), x[mp:].reshape(bs, ts, d), *stack(outs_p), *stack(outs_s))
```

```python
import functools

import jax
import jax.numpy as jnp
from jax import lax
from jax.experimental import pallas as pl
from jax.experimental.pallas import tpu as pltpu

f32 = jnp.float32
bf16 = jnp.bfloat16
i32 = jnp.int32
i16 = jnp.int16

CHUNK = 64
A_HEADS, A_KV_HEADS, A_HEAD_DIM = 8, 2, 128
A_GROUP = A_HEADS // A_KV_HEADS
A_WIDTH = A_HEADS * A_HEAD_DIM
KV_WIDTH = A_KV_HEADS * A_HEAD_DIM
IDX_HEADS, IDX_DIM = 8, 64
TOPK_MAX = 256
B_HEADS, B_KEY_DIM, B_VAL_DIM = 8, 128, 128
B_QK_WIDTH = B_HEADS * B_KEY_DIM
B_WIDTH = B_HEADS * B_VAL_DIM
CONV_WIDTH = 4
CONV_CH = 2 * B_QK_WIDTH + B_WIDTH
N_MOD = 9
FFN_WEIGHT = 0.5
EPS = 1e-6

COL_QKVB, COL_QA, COL_GA, COL_GB, COL_ZB, COL_KV, COL_QI, COL_SM = 0, 3072, 4096, 6144, 8192, 9216, 9728, 10240
SM_WIDTH = 128
SM_WI, SM_AB, SM_BB = 64, 72, 80
PROJ_COLS = COL_SM + SM_WIDTH

INT_MIN = -(2 ** 31)
INT_MAX = 2 ** 31 - 1
I16_MIN, I16_MAX = -(2 ** 15), 2 ** 15 - 1
NEG = -1e30
MASKED_BF16_BITS = -3638
LANES = 128
MXU_DEPTH = 256
LOG2E = 1.4426950408889634
DSA_TK = 1408
SWEEP_BLOCKS = 1
ATT_ROWS = 64
VMEM_LIMIT = 56 * 1024 * 1024


def _pick(n, prefs):
    for p in prefs:
        if n % p == 0:
            return p
    raise ValueError(f"no tile of {prefs} divides {n}")


def _params(*sem):
    return pltpu.CompilerParams(dimension_semantics=sem, vmem_limit_bytes=VMEM_LIMIT)


def _rms(y, g):
    return y * lax.rsqrt(jnp.mean(y * y, axis=-1, keepdims=True) + EPS) * g


def _ada_kernel(c_ref, w_ref, b_ref, o_ref):
    o_ref[...] = jnp.dot(c_ref[...].astype(bf16), w_ref[...].astype(bf16),
                         preferred_element_type=f32) + b_ref[...]


def _ada(c, w, b):
    m, d = c.shape
    n = w.shape[1]
    tn = _pick(n, (1024, 512, 256, 128))
    return pl.pallas_call(
        _ada_kernel,
        grid=(n // tn,),
        in_specs=[pl.BlockSpec((m, d), lambda j: (0, 0)),
                  pl.BlockSpec((d, tn), lambda j: (0, j)),
                  pl.BlockSpec((1, tn), lambda j: (0, j))],
        out_specs=pl.BlockSpec((m, tn), lambda j: (0, j)),
        out_shape=jax.ShapeDtypeStruct((m, n), f32),
        compiler_params=_params("arbitrary"),
        name="ada",
    )(c, w, b)


def _norm_mod_kernel(x_ref, mod_ref, g_ref, o_ref, *, sh, sc, nb):
    g = g_ref[...]
    for j in range(nb):
        rows = slice(j * CHUNK, (j + 1) * CHUNK)
        y = _rms(x_ref[rows, :], g)
        o_ref[rows, :] = (y * (1.0 + mod_ref[j, sc:sc + 1, :]) + mod_ref[j, sh:sh + 1, :]).astype(bf16)


def _norm_mod(x, modg, g, sh, sc):
    m, d = x.shape
    tm = _pick(m, (512, 256, 128, 64))
    nb = tm // CHUNK
    return pl.pallas_call(
        functools.partial(_norm_mod_kernel, sh=sh, sc=sc, nb=nb),
        grid=(m // tm,),
        in_specs=[pl.BlockSpec((tm, d), lambda i: (i, 0)),
                  pl.BlockSpec((nb, N_MOD, d), lambda i: (i, 0, 0)),
                  pl.BlockSpec((1, d), lambda i: (0, 0))],
        out_specs=pl.BlockSpec((tm, d), lambda i: (i, 0)),
        out_shape=jax.ShapeDtypeStruct((m, d), bf16),
        compiler_params=_params("arbitrary"),
        name="norm_mod",
    )(x, modg, g)


def _gateup_kernel(h_ref, wg_ref, wu_ref, o_ref):
    h = h_ref[...]
    a = jnp.dot(h, wg_ref[...], preferred_element_type=f32)
    u = jnp.dot(h, wu_ref[...], preferred_element_type=f32)
    o_ref[...] = (a * jax.nn.sigmoid(a) * u).astype(bf16)


def _gateup(h, wg, wu):
    m, d = h.shape
    n = wg.shape[1]
    tm = _pick(m, (1024, 512, 256, 128, 64))
    tn = _pick(n, (512, 256, 128))
    return pl.pallas_call(
        _gateup_kernel,
        grid=(m // tm, n // tn),
        in_specs=[pl.BlockSpec((tm, d), lambda i, j: (i, 0)),
                  pl.BlockSpec((d, tn), lambda i, j: (0, j)),
                  pl.BlockSpec((d, tn), lambda i, j: (0, j))],
        out_specs=pl.BlockSpec((tm, tn), lambda i, j: (i, j)),
        out_shape=jax.ShapeDtypeStruct((m, n), bf16),
        compiler_params=_params("arbitrary", "arbitrary"),
        name="gateup",
    )(h, wg, wu)


def _mm_norm_res_kernel(a_ref, w_ref, x_ref, mod_ref, g_ref, o_ref, y_scr, *, gt, coeff, nb, nj, tn):
    j = pl.program_id(1)
    y_scr[:, pl.ds(pl.multiple_of(j * tn, tn), tn)] = jnp.dot(a_ref[...], w_ref[...], preferred_element_type=f32)

    @pl.when(j == nj - 1)
    def _():
        g = g_ref[...]
        for jj in range(nb):
            rows = slice(jj * CHUNK, (jj + 1) * CHUNK)
            gate = mod_ref[jj, gt:gt + 1, :]
            if coeff is not None:
                gate = coeff * gate
            o_ref[rows, :] = x_ref[rows, :] + gate * _rms(y_scr[rows, :], g)


def _mm_norm_res(a, w, x, modg, g, gt, coeff):
    m, kd = a.shape
    d = w.shape[1]
    tm = _pick(m, (512, 256, 128, 64))
    tn = _pick(d, (512, 256, 128))
    nb, nj = tm // CHUNK, d // tn
    return pl.pallas_call(
        functools.partial(_mm_norm_res_kernel, gt=gt, coeff=coeff, nb=nb, nj=nj, tn=tn),
        grid=(m // tm, nj),
        in_specs=[pl.BlockSpec((tm, kd), lambda i, j: (i, 0)),
                  pl.BlockSpec((kd, tn), lambda i, j: (0, j)),
                  pl.BlockSpec((tm, d), lambda i, j: (i, 0)),
                  pl.BlockSpec((nb, N_MOD, d), lambda i, j: (i, 0, 0)),
                  pl.BlockSpec((1, d), lambda i, j: (0, 0))],
        out_specs=pl.BlockSpec((tm, d), lambda i, j: (i, 0)),
        out_shape=jax.ShapeDtypeStruct((m, d), f32),
        scratch_shapes=[pltpu.VMEM((tm, d), f32)],
        compiler_params=_params("arbitrary", "arbitrary"),
        name="mm_norm_res",
    )(a, w, x, modg, g)


def _mm_kernel(a_ref, w_ref, o_ref):
    o_ref[...] = jnp.dot(a_ref[...], w_ref[...], preferred_element_type=f32)


def _mm(a, w):
    m, kd = a.shape
    n = w.shape[1]
    tm = _pick(m, (1024, 512, 256, 128, 64))
    tn = _pick(n, (1152, 1024, 512, 384, 256, 128))
    return pl.pallas_call(
        _mm_kernel,
        grid=(m // tm, n // tn),
        in_specs=[pl.BlockSpec((tm, kd), lambda i, j: (i, 0)),
                  pl.BlockSpec((kd, tn), lambda i, j: (0, j))],
        out_specs=pl.BlockSpec((tm, tn), lambda i, j: (i, j)),
        out_shape=jax.ShapeDtypeStruct((m, n), f32),
        compiler_params=_params("arbitrary", "arbitrary"),
        name="in_proj",
    )(a, w)


def _kidx_kernel(sm_ref, g_ref, b_ref, o_ref):
    x = sm_ref[:, 0:IDX_DIM]
    xc = x - jnp.mean(x, axis=-1, keepdims=True)
    y = xc * lax.rsqrt(jnp.mean(xc * xc, axis=-1, keepdims=True) + EPS)
    o_ref[...] = y * g_ref[...] + b_ref[...]


def _kidx(proj, g, b):
    m = proj.shape[0]
    tm = _pick(m, (1024, 512, 256, 128, 64))
    return pl.pallas_call(
        _kidx_kernel,
        grid=(m // tm,),
        in_specs=[pl.BlockSpec((tm, SM_WIDTH), lambda i: (i, COL_SM // SM_WIDTH)),
                  pl.BlockSpec((1, IDX_DIM), lambda i: (0, 0)),
                  pl.BlockSpec((1, IDX_DIM), lambda i: (0, 0))],
        out_specs=pl.BlockSpec((tm, IDX_DIM), lambda i: (i, 0)),
        out_shape=jax.ShapeDtypeStruct((m, IDX_DIM), f32),
        compiler_params=_params("arbitrary"),
        name="kidx_ln",
    )(proj, g, b)


def _dsa_kernel(qi_ref, sm_ref, qa_ref, kit_ref, kt_ref, v_ref, o_ref,
                hi_ref, lo_ref, w_scr, q_scr, p_scr, m_scr, acc_scr, *, tq, tk, past, n_keys, topk, nbits):
    chunk_shift = CHUNK.bit_length() - 1
    nlc = tk // LANES
    qpos0 = past + pl.program_id(1) * tq
    limit = jnp.minimum((((qpos0 + tq - 1) >> chunk_shift) + 1) * CHUNK, n_keys)
    nkb = (limit + tk - 1) // tk
    row = lax.broadcasted_iota(i32, (tq, LANES), 0)
    lane = lax.broadcasted_iota(i32, (tq, LANES), 1)
    qchunk = (qpos0 + row) >> chunk_shift

    def lane_chunks(kb, width=tk):
        c0 = pl.multiple_of(kb * width, width)
        return c0, [pl.multiple_of(c0 + c * LANES, LANES) for c in range(width // LANES)]

    nsb = (nkb + SWEEP_BLOCKS - 1) // SWEEP_BLOCKS
    sweep_chunks = lambda sb: lane_chunks(sb, SWEEP_BLOCKS * tk)[1]

    qi = qi_ref[...].astype(bf16)
    qis = [qi[:, h * IDX_DIM:(h + 1) * IDX_DIM] for h in range(IDX_HEADS)]
    w = sm_ref[:, SM_WI:SM_WI + IDX_HEADS] * (IDX_HEADS ** -0.5 * IDX_DIM ** -0.5)
    for h in range(IDX_HEADS):
        w_scr[h] = jnp.broadcast_to(w[:, h:h + 1], (tq, LANES))

    def score_body(masked, kb, carry):
        c0, ccs = lane_chunks(kb)
        kit = kit_ref[:, pl.ds(c0, tk)]
        sc = [jnp.zeros((tq, LANES), f32)] * nlc
        for h in range(IDX_HEADS):
            s = jnp.dot(qis[h], kit, preferred_element_type=f32)
            wh = w_scr[h]
            sc = [sc[c] + wh * jnp.maximum(s[:, c * LANES:(c + 1) * LANES], 0.0) for c in range(nlc)]
        for c, cc in enumerate(ccs):
            bits = lax.bitcast_convert_type(sc[c], i32)
            key = jnp.where(bits < 0, INT_MIN - bits, bits)
            if masked:
                kpos = cc + lane
                allowed = ((kpos >> chunk_shift) <= qchunk) & (kpos < n_keys)
                key = jnp.where(allowed, key, INT_MIN)
            hi_ref[:, pl.ds(cc, LANES)] = (key >> 16).astype(i16)
            lo_ref[:, pl.ds(cc, LANES)] = (key ^ 0x8000).astype(i16)
        return carry

    nfull = jnp.minimum((qpos0 >> chunk_shift) * CHUNK, n_keys) // tk
    lax.fori_loop(0, nfull, functools.partial(score_body, False), 0)
    lax.fori_loop(nfull, nkb, functools.partial(score_body, True), 0)

    def fill_body(kb, carry):
        for cc in lane_chunks(kb)[1]:
            hi_ref[:, pl.ds(cc, LANES)] = jnp.full((tq, LANES), I16_MIN, i16)
            lo_ref[:, pl.ds(cc, LANES)] = jnp.full((tq, LANES), I16_MIN, i16)
        return carry

    lax.fori_loop(nkb, nsb * SWEEP_BLOCKS, fill_body, 0)

    def count16(pred):
        def body(sb, acc):
            for cc in sweep_chunks(sb):
                acc = acc + jnp.where(pred(cc), jnp.int16(1), jnp.int16(0))
            return acc
        acc = lax.fori_loop(0, nsb, body, jnp.zeros((tq, LANES), i16))
        return jnp.broadcast_to(jnp.sum(acc.astype(f32), axis=1, keepdims=True), (tq, LANES))

    hi_at = lambda cc: hi_ref[:, pl.ds(cc, LANES)]
    lo_at = lambda cc: lo_ref[:, pl.ds(cc, LANES)]
    kpos16 = lambda cc: (cc + lane).astype(i16)
    kf = float(topk)

    def bisect16(at, ct0):
        def bit_body(it, carry):
            t, ct = carry
            cand = t + lax.shift_left(jnp.int32(1), 15 - it)
            c16 = cand.astype(i16)
            cnt = count16(lambda cc: at(cc) >= c16)
            take = cnt >= kf
            return jnp.where(take, cand, t), jnp.where(take, cnt, ct)
        return lax.fori_loop(0, 16, bit_body, (jnp.full((tq, LANES), I16_MIN, i32), ct0))

    th, ct = bisect16(hi_at, jnp.zeros((tq, LANES), f32) + (nsb * (SWEEP_BLOCKS * tk)).astype(f32))
    th16 = th.astype(i16)

    def mask_body(sb, carry):
        for cc in sweep_chunks(sb):
            hi = hi_at(cc)
            lo_ref[:, pl.ds(cc, LANES)] = jnp.where(hi > th16, jnp.int16(I16_MAX),
                                                    jnp.where(hi == th16, lo_at(cc), jnp.int16(I16_MIN)))
        return carry

    lax.fori_loop(0, nsb, mask_body, 0)
    tl, ct = bisect16(lo_at, ct)
    tl16 = tl.astype(i16)

    def tie_cut():
        r = kf - count16(lambda cc: (hi_at(cc) > th16) | ((hi_at(cc) == th16) & (lo_at(cc) > tl16)))

        def j_body(it, j):
            cand = j + lax.shift_left(jnp.int32(1), nbits - 1 - it)
            c16 = cand.astype(i16)
            cnt = count16(lambda cc: (hi_at(cc) == th16) & (lo_at(cc) == tl16) & (kpos16(cc) < c16))
            return jnp.where(cnt < r, cand, j)

        return lax.fori_loop(0, nbits, j_body, jnp.zeros((tq, LANES), i32))

    jcut = lax.cond(jnp.max(ct) > kf, tie_cut, lambda: jnp.full((tq, LANES), I16_MAX, i32)).astype(i16)
    tls16 = jnp.where(th == I16_MIN, jnp.maximum(tl, I16_MIN + 1), tl).astype(i16)

    def bias_body(sb, carry):
        for cc in sweep_chunks(sb):
            hi, lo = hi_at(cc), lo_at(cc)
            sel = (hi > th16) | ((hi == th16) & ((lo > tls16) | ((lo == tls16) & (kpos16(cc) <= jcut))))
            hi_ref[:, pl.ds(cc, LANES)] = jnp.where(sel, jnp.int16(0), jnp.int16(MASKED_BF16_BITS))
        return carry

    lax.fori_loop(0, nsb, bias_body, 0)

    qa = qa_ref[...] * (A_HEAD_DIM ** -0.5 * LOG2E)
    for n in range(A_KV_HEADS):
        q_scr[n] = jnp.concatenate(
            [qa[:, (n * A_GROUP + g) * A_HEAD_DIM:(n * A_GROUP + g + 1) * A_HEAD_DIM] for g in range(A_GROUP)],
            axis=0).astype(bf16)
    m_scr[...] = jnp.full(m_scr.shape, NEG, f32)
    acc_scr[...] = jnp.zeros(acc_scr.shape, f32)

    def att_body(kb, carry):
        c0, ccs = lane_chunks(kb)
        bias = [lax.bitcast_convert_type(hi_at(cc), bf16).astype(f32) for cc in ccs]
        heads = range(A_KV_HEADS)
        ar = min(ATT_ROWS, tq)
        blocks = [(slice(blk * ar, (blk + 1) * ar), slice(blk * ar % tq, blk * ar % tq + ar))
                  for blk in range(A_GROUP * tq // ar)]
        s_all = [jnp.dot(q_scr[n], kt_ref[n * A_HEAD_DIM:(n + 1) * A_HEAD_DIM, pl.ds(c0, tk)],
                         preferred_element_type=f32) for n in heads]
        m_news, alphas = [[] for _ in heads], [[] for _ in heads]
        for n in heads:
            for rows, qrows in blocks:
                mx = s_all[n][rows, 0:LANES] + bias[0][qrows]
                for c in range(1, nlc):
                    mx = jnp.maximum(mx, s_all[n][rows, c * LANES:(c + 1) * LANES] + bias[c][qrows])
                m_old = m_scr[n, rows]
                m_new = jnp.maximum(m_old, jnp.max(mx, axis=1, keepdims=True))
                m_scr[n, rows] = m_new
                m_news[n].append(m_new)
                alphas[n].append(jnp.exp2(m_old - m_new))
        for n in heads:
            for blk, (rows, qrows) in enumerate(blocks):
                for c in range(nlc):
                    p = jnp.exp2(s_all[n][rows, c * LANES:(c + 1) * LANES] - (m_news[n][blk] - bias[c][qrows]))
                    p_scr[n, rows, c * LANES:(c + 1) * LANES] = p.astype(bf16)
        for n in heads:
            vv = v_ref[pl.ds(c0, tk), n * 2 * A_HEAD_DIM:(n + 1) * 2 * A_HEAD_DIM]
            alpha = jnp.concatenate(alphas[n], axis=0)
            acc_scr[n] = (jnp.concatenate([alpha, alpha], axis=1) * acc_scr[n]
                          + jnp.dot(p_scr[n], vv, preferred_element_type=f32))
        return carry

    lax.fori_loop(0, nkb, att_body, 0)

    for n in range(A_KV_HEADS):
        o = acc_scr[n, :, 0:A_HEAD_DIM] / acc_scr[n, :, A_HEAD_DIM:2 * A_HEAD_DIM]
        for g in range(A_GROUP):
            hh = n * A_GROUP + g
            o_ref[:, hh * A_HEAD_DIM:(hh + 1) * A_HEAD_DIM] = o[g * tq:(g + 1) * tq].astype(bf16)


def _dsa(proj, row0, nbatch, t_len, kit, kt, v, *, tk, past, n_keys, topk):
    lp = v.shape[1]
    tq = _pick(t_len, (128, 64))
    assert lp % tk == 0 and tk % LANES == 0 and row0 % tq == 0 and past % CHUNK == 0
    assert lp + SWEEP_BLOCKS * tk <= I16_MAX and tq % min(ATT_ROWS, tq) == 0
    nq = t_len // tq
    rb = row0 // tq
    rows = A_GROUP * tq
    wide = -(-lp // (SWEEP_BLOCKS * tk)) * (SWEEP_BLOCKS * tk)
    wide += LANES * (1 - wide // LANES % 2)

    def qmap(width_col):
        return lambda b, i: (rb + b * nq + i, width_col)

    resident = dict(pipeline_mode=pl.Buffered(1)) if nbatch == 1 else {}
    return pl.pallas_call(
        functools.partial(_dsa_kernel, tq=tq, tk=tk, past=past, n_keys=n_keys, topk=topk,
                          nbits=wide.bit_length()),
        grid=(nbatch, nq),
        in_specs=[pl.BlockSpec((tq, IDX_HEADS * IDX_DIM), qmap(COL_QI // (IDX_HEADS * IDX_DIM))),
                  pl.BlockSpec((tq, SM_WIDTH), qmap(COL_SM // SM_WIDTH)),
                  pl.BlockSpec((tq, A_WIDTH), qmap(COL_QA // A_WIDTH)),
                  pl.BlockSpec((None, IDX_DIM, lp), lambda b, i: (b, 0, 0), **resident),
                  pl.BlockSpec((None, KV_WIDTH, lp), lambda b, i: (b, 0, 0), **resident),
                  pl.BlockSpec((None, lp, 2 * KV_WIDTH), lambda b, i: (b, 0, 0), **resident)],
        out_specs=pl.BlockSpec((tq, A_WIDTH), lambda b, i: (b * nq + i, 0)),
        out_shape=jax.ShapeDtypeStruct((nbatch * t_len, A_WIDTH), bf16),
        scratch_shapes=[pltpu.VMEM((tq, wide), i16),
                        pltpu.VMEM((tq, wide), i16),
                        pltpu.VMEM((IDX_HEADS, tq, LANES), f32),
                        pltpu.VMEM((A_KV_HEADS, rows, A_HEAD_DIM), bf16),
                        pltpu.VMEM((A_KV_HEADS, rows, tk), bf16),
                        pltpu.VMEM((A_KV_HEADS, rows, LANES), f32),
                        pltpu.VMEM((A_KV_HEADS, rows, 2 * A_HEAD_DIM), f32)],
        compiler_params=_params("arbitrary", "arbitrary"),
        name="dsa",
    )(proj, proj, proj, kit, kt, v)


_NN = (((1,), (0,)), ((), ()))
_NT = (((1,), (1,)), ((), ()))
_TN = (((0,), (0,)), ((), ()))


def _split(a):
    hi = a.astype(bf16)
    return hi, (a - hi.astype(f32)).astype(bf16)


def _dot1(a, b, dims=_NN):
    return lax.dot_general(a.astype(bf16), b.astype(bf16), dims, preferred_element_type=f32)


def _dot3_sq(a, b):
    ah, al = _split(a)
    bh, bl = _split(b)
    pad = MXU_DEPTH - 3 * CHUNK
    lhs = jnp.concatenate([ah, ah, al, jnp.zeros((CHUNK, pad), bf16)], axis=1)
    rhs = jnp.concatenate([bh, bl, bh, jnp.zeros((pad, CHUNK), bf16)], axis=0)
    return jnp.dot(lhs, rhs, preferred_element_type=f32)


def _gdn_kernel(qkv_ref, sm_ref, z_ref, cw_ref, alog_ref, dtb_ref, gn_ref, buf_ref, s0_ref,
                ob_ref, sout_ref, xp_scr, s_scr, *, nc):
    c = pl.program_id(1)
    chunk_shift = CHUNK.bit_length() - 1

    @pl.when(c == 0)
    def _():
        xp_scr[0:8, :] = buf_ref[...]
        s_scr[...] = s0_ref[...]

    x = qkv_ref[...]
    xp_scr[8:8 + CHUNK, :] = x
    cw = cw_ref[...]
    y = (cw[0:1] * xp_scr[5:5 + CHUNK, :] + cw[1:2] * xp_scr[6:6 + CHUNK, :]
         + cw[2:3] * xp_scr[7:7 + CHUNK, :] + cw[3:4] * x)
    xp_scr[0:8, :] = x[CHUNK - 8:CHUNK, :]
    y = y * jax.nn.sigmoid(y)

    sm = sm_ref[...]
    z = sm + dtb_ref[...]
    softplus = jnp.maximum(z, 0.0) + jnp.log1p(jnp.exp(-jnp.abs(z)))
    g_all = -jnp.exp(alog_ref[...]) * softplus
    beta_all = jax.nn.sigmoid(sm)
    r = lax.broadcasted_iota(i32, (CHUNK, CHUNK), 0)
    cc = lax.broadcasted_iota(i32, (CHUNK, CHUNK), 1)
    tri, strict = r >= cc, r > cc
    eye = jnp.where(r == cc, 1.0, 0.0)
    gc = lax.dot_general(jnp.where(tri, 1.0, 0.0), g_all, _NN,
                         precision=lax.Precision.HIGHEST, preferred_element_type=f32)
    gc_t = gc.T
    g_last = gc[CHUNK - 1:CHUNK, :]
    e_gc, e_last, e_rest = jnp.exp(gc), jnp.exp(g_last), jnp.exp(g_last - gc)
    gn = gn_ref[...]

    heads = range(B_HEADS)
    hs = [slice(h * B_KEY_DIM, (h + 1) * B_KEY_DIM) for h in heads]
    col = lambda a, h: a[:, SM_AB + h:SM_AB + h + 1]
    qs, ks, vbs, kbs, decays = [], [], [], [], []
    for h in heads:
        q = y[:, hs[h]]
        k = y[:, B_QK_WIDTH + h * B_KEY_DIM:B_QK_WIDTH + (h + 1) * B_KEY_DIM]
        v = y[:, 2 * B_QK_WIDTH + h * B_VAL_DIM:2 * B_QK_WIDTH + (h + 1) * B_VAL_DIM]
        beta = beta_all[:, SM_BB + h:SM_BB + h + 1]
        qs.append(q * lax.rsqrt(jnp.sum(q * q, axis=-1, keepdims=True) + EPS) * (B_KEY_DIM ** -0.5))
        k = k * lax.rsqrt(jnp.sum(k * k, axis=-1, keepdims=True) + EPS)
        ks.append(k)
        kbs.append(k * beta)
        vbs.append(v * beta)
        diff = col(gc, h) - gc_t[SM_AB + h:SM_AB + h + 1, :]
        decays.append(jnp.where(tri, jnp.exp(jnp.where(tri, diff, 0.0)), 0.0))
    lows = [jnp.where(strict, _dot1(kbs[h], ks[h], _NT) * decays[h], 0.0) for h in heads]
    xpows, ainvs = lows, [eye - low for low in lows]
    for _ in range(chunk_shift - 1):
        xpows = [_dot3_sq(xp, xp) for xp in xpows]
        ainvs = [ainvs[h] + _dot3_sq(ainvs[h], xpows[h]) for h in heads]
    sols = [_dot1(ainvs[h], jnp.concatenate([vbs[h], kbs[h] * col(e_gc, h)], axis=1)) for h in heads]
    qks = [_dot1(qs[h], ks[h], _NT) * decays[h] for h in heads]
    states = [s_scr[h] for h in heads]
    v_news = [sols[h][:, :B_VAL_DIM] - _dot1(sols[h][:, B_VAL_DIM:], states[h]) for h in heads]
    outs = [_dot1(qs[h] * col(e_gc, h), states[h]) + _dot1(qks[h], v_news[h]) for h in heads]
    for h in heads:
        s_scr[h] = states[h] * col(e_last, h) + _dot1(ks[h] * col(e_rest, h), v_news[h], _TN)
    for h in heads:
        zh = z_ref[:, hs[h]]
        ob_ref[:, hs[h]] = (_rms(outs[h], gn) * (zh * jax.nn.sigmoid(zh))).astype(bf16)

    @pl.when(c == nc - 1)
    def _():
        sout_ref[...] = s_scr[...]


def _gdn(proj, row0, nbatch, t_len, conv_w, alog_row, dtb_row, gn, buf8, s0):
    assert t_len % CHUNK == 0 and row0 % CHUNK == 0
    nc = t_len // CHUNK
    rb = row0 // CHUNK

    def rmap(col):
        return lambda b, c: (rb + b * nc + c, col)

    return pl.pallas_call(
        functools.partial(_gdn_kernel, nc=nc),
        grid=(nbatch, nc),
        in_specs=[pl.BlockSpec((CHUNK, CONV_CH), rmap(COL_QKVB // CONV_CH)),
                  pl.BlockSpec((CHUNK, SM_WIDTH), rmap(COL_SM // SM_WIDTH)),
                  pl.BlockSpec((CHUNK, B_WIDTH), rmap(COL_ZB // B_WIDTH)),
                  pl.BlockSpec((CONV_WIDTH, CONV_CH), lambda b, c: (0, 0)),
                  pl.BlockSpec((1, SM_WIDTH), lambda b, c: (0, 0)),
                  pl.BlockSpec((1, SM_WIDTH), lambda b, c: (0, 0)),
                  pl.BlockSpec((1, B_VAL_DIM), lambda b, c: (0, 0)),
                  pl.BlockSpec((None, 8, CONV_CH), lambda b, c: (b, 0, 0)),
                  pl.BlockSpec((None, B_HEADS, B_KEY_DIM, B_VAL_DIM), lambda b, c: (b, 0, 0, 0))],
        out_specs=[pl.BlockSpec((CHUNK, B_WIDTH), lambda b, c: (b * nc + c, 0)),
                   pl.BlockSpec((None, B_HEADS, B_KEY_DIM, B_VAL_DIM), lambda b, c: (b, 0, 0, 0))],
        out_shape=[jax.ShapeDtypeStruct((nbatch * t_len, B_WIDTH), bf16),
                   jax.ShapeDtypeStruct((nbatch, B_HEADS, B_KEY_DIM, B_VAL_DIM), f32)],
        scratch_shapes=[pltpu.VMEM((8 + CHUNK, CONV_CH), f32),
                        pltpu.VMEM((B_HEADS, B_KEY_DIM, B_VAL_DIM), f32)],
        compiler_params=_params("arbitrary", "arbitrary"),
        name="gdn",
    )(proj, proj, proj, conv_w, alog_row, dtb_row, gn, buf8, s0)


def _merge_kernel(oa_ref, ob_ref, wa_ref, wb_ref, ga_ref, gb_ref, o_ref):
    a = jnp.dot(oa_ref[...], wa_ref[...], preferred_element_type=f32)
    b = jnp.dot(ob_ref[...], wb_ref[...], preferred_element_type=f32)
    o_ref[...] = (jax.nn.sigmoid(ga_ref[...]) * a + jax.nn.sigmoid(gb_ref[...]) * b).astype(bf16)


def _merge(oa, ob, wa, wb, proj):
    m = oa.shape[0]
    d = wa.shape[1]
    tm = _pick(m, (1024, 512, 256, 128, 64))
    tn = 512
    assert d % tn == 0 and COL_GA % tn == 0 and COL_GB % tn == 0
    return pl.pallas_call(
        _merge_kernel,
        grid=(m // tm, d // tn),
        in_specs=[pl.BlockSpec((tm, A_WIDTH), lambda i, j: (i, 0)),
                  pl.BlockSpec((tm, B_WIDTH), lambda i, j: (i, 0)),
                  pl.BlockSpec((A_WIDTH, tn), lambda i, j: (0, j)),
                  pl.BlockSpec((B_WIDTH, tn), lambda i, j: (0, j)),
                  pl.BlockSpec((tm, tn), lambda i, j: (i, COL_GA // tn + j)),
                  pl.BlockSpec((tm, tn), lambda i, j: (i, COL_GB // tn + j))],
        out_specs=pl.BlockSpec((tm, tn), lambda i, j: (i, j)),
        out_shape=jax.ShapeDtypeStruct((m, d), bf16),
        compiler_params=_params("arbitrary", "arbitrary"),
        name="merge",
    )(oa, ob, wa, wb, proj, proj)


def _reorder_w_in(w):
    o, pts = 0, {}
    for name, width in (("qa", A_WIDTH), ("ka", KV_WIDTH), ("va", KV_WIDTH), ("qi", IDX_HEADS * IDX_DIM),
                        ("ki", IDX_DIM), ("wi", IDX_HEADS), ("qb", B_QK_WIDTH), ("kb", B_QK_WIDTH),
                        ("vb", B_WIDTH), ("zb", B_WIDTH), ("ab", B_HEADS), ("bb", B_HEADS)):
        pts[name] = (o, o + width)
        o += width
    d = w.shape[0]
    rest = w.shape[1] - o
    assert rest == 2 * d
    seg = lambda a, b: w[:, pts[a][0]:pts[b][1]]
    used = IDX_DIM + IDX_HEADS + 2 * B_HEADS
    out = jnp.concatenate([seg("qb", "vb"), seg("qa", "qa"), w[:, o:o + d], w[:, o + d:o + 2 * d], seg("zb", "zb"),
                           seg("ka", "va"), seg("qi", "qi"), seg("ki", "wi"), seg("ab", "bb"),
                           jnp.zeros((d, SM_WIDTH - used), w.dtype)], axis=1)
    assert out.shape[1] == PROJ_COLS and COL_GA == COL_QA + A_WIDTH and COL_GB == COL_GA + d
    return out.astype(bf16)


def _lane_row(vals, start):
    return jnp.zeros((1, SM_WIDTH), f32).at[0, start:start + vals.shape[0]].set(vals.astype(f32))


def kernel(x_prompt, x_sample, cache_k, cache_v, cache_kidx, state_conv, state_gdn, c_prompt, c_sample, w_ada, b_ada, norm_g, w1_gate, w1_up, w1_down, w_in, idx_ln_g, idx_ln_b, conv_w, a_log, dt_bias, gdn_norm_g, w_branch_a, w_branch_b, w_out, w2_gate, w2_up, w2_down):
    bp, tp, d = x_prompt.shape
    bs, ts, _ = x_sample.shape
    depth = w_ada.shape[0]
    past = cache_k.shape[2]
    assert tp % CHUNK == 0 and ts % CHUNK == 0 and ts >= CONV_WIDTH - 1 and tp >= CONV_WIDTH - 1
    mp, ms = bp * tp, bs * ts

    x = jnp.concatenate([x_prompt.reshape(mp, d), x_sample.reshape(ms, d)], axis=0)
    c_all = jnp.concatenate([c_prompt, c_sample], axis=0)
    c_pad = jnp.pad(c_all, ((0, -c_all.shape[0] % 8), (0, 0)))

    outs_p, outs_s = [], []
    for l in range(depth):
        mod = _ada(c_pad, w_ada[l], b_ada[l][None])[:bp + bs].reshape(bp + bs, N_MOD, d)
        modg = jnp.concatenate([jnp.repeat(mod[:bp], tp // CHUNK, axis=0),
                                jnp.repeat(mod[bp:], ts // CHUNK, axis=0)], axis=0)
        ng = lambda i: norm_g[l, i][None]

        h = _norm_mod(x, modg, ng(0), 0, 1)
        act = _gateup(h, w1_gate[l].astype(bf16), w1_up[l].astype(bf16))
        x = _mm_norm_res(act, w1_down[l].astype(bf16), x, modg, ng(1), 2, FFN_WEIGHT)

        h = _norm_mod(x, modg, ng(2), 3, 4)
        proj = _mm(h, _reorder_w_in(w_in[l]))
        kidx = _kidx(proj, idx_ln_g[l][None], idx_ln_b[l][None])

        kv_p = proj[:mp, COL_KV:COL_KV + 2 * KV_WIDTH].reshape(bp, tp, 2 * KV_WIDTH)
        k_p, v_p = kv_p[..., :KV_WIDTH], kv_p[..., KV_WIDTH:]
        ki_p = kidx[:mp].reshape(bp, tp, IDX_DIM)
        kv_s = proj[mp:, COL_KV:COL_KV + 2 * KV_WIDTH].reshape(bs, ts, 2 * KV_WIDTH)
        k_s, v_s = kv_s[..., :KV_WIDTH], kv_s[..., KV_WIDTH:]
        ki_s = kidx[mp:].reshape(bs, ts, IDX_DIM)

        def keys(k_all, v_all, ki_all):
            n = k_all.shape[1]
            lp = -(-n // DSA_TK) * DSA_TK
            lp += DSA_TK * (1 - lp // LANES % 2)
            pad = ((0, 0), (0, lp - n), (0, 0))
            kt = jnp.swapaxes(jnp.pad(k_all.astype(bf16), pad), 1, 2)
            kit = jnp.swapaxes(jnp.pad(ki_all.astype(bf16), pad), 1, 2)
            vb = jnp.pad(v_all.astype(bf16), pad)
            ones = jnp.ones(vb.shape[:2] + (A_HEAD_DIM,), bf16)
            v_ext = jnp.concatenate([x for n in range(A_KV_HEADS)
                                     for x in (vb[..., n * A_HEAD_DIM:(n + 1) * A_HEAD_DIM], ones)], axis=-1)
            return kit, kt, v_ext, n

        kit, kt, vv, n_keys = keys(k_p, v_p, ki_p)
        oa_p = _dsa(proj, 0, bp, tp, kit, kt, vv, tk=DSA_TK, past=0, n_keys=n_keys,
                    topk=min(TOPK_MAX, n_keys // 4))
        kit, kt, vv, n_keys = keys(
            jnp.concatenate([cache_k[l].reshape(bs, past, KV_WIDTH), k_s], axis=1),
            jnp.concatenate([cache_v[l].reshape(bs, past, KV_WIDTH), v_s], axis=1),
            jnp.concatenate([cache_kidx[l], ki_s], axis=1))
        oa_s = _dsa(proj, mp, bs, ts, kit, kt, vv, tk=DSA_TK, past=past, n_keys=n_keys,
                    topk=min(TOPK_MAX, n_keys // 4))

        alog_row, dtb_row = _lane_row(a_log[l], SM_AB), _lane_row(dt_bias[l], SM_AB)
        gn = gdn_norm_g[l][None]
        ob_p, gdn_p = _gdn(proj, 0, bp, tp, conv_w[l], alog_row, dtb_row, gn,
                           jnp.zeros((bp, 8, CONV_CH), f32), jnp.zeros((bp, B_HEADS, B_KEY_DIM, B_VAL_DIM), f32))
        buf8 = jnp.pad(state_conv[l], ((0, 0), (8 - (CONV_WIDTH - 1), 0), (0, 0)))
        ob_s, gdn_s = _gdn(proj, mp, bs, ts, conv_w[l], alog_row, dtb_row, gn, buf8, state_gdn[l])

        merged = _merge(jnp.concatenate([oa_p, oa_s], axis=0), jnp.concatenate([ob_p, ob_s], axis=0),
                        w_branch_a[l].astype(bf16), w_branch_b[l].astype(bf16), proj)
        x = _mm_norm_res(merged, w_out[l].astype(bf16), x, modg, ng(3), 5, None)

        h = _norm_mod(x, modg, ng(4), 6, 7)
        act = _gateup(h, w2_gate[l].astype(bf16), w2_up[l].astype(bf16))
        x = _mm_norm_res(act, w2_down[l].astype(bf16), x, modg, ng(5), 8, FFN_WEIGHT)

        tail = CONV_WIDTH - 1
        conv_p = proj[:mp, :CONV_CH].reshape(bp, tp, CONV_CH)[:, tp - tail:]
        conv_s = proj[mp:, :CONV_CH].reshape(bs, ts, CONV_CH)[:, ts - tail:]
        outs_p.append((k_p.reshape(bp, tp, A_KV_HEADS, A_HEAD_DIM), v_p.reshape(bp, tp, A_KV_HEADS, A_HEAD_DIM),
                       ki_p, conv_p, gdn_p))
        outs_s.append((k_s.reshape(bs, ts, A_KV_HEADS, A_HEAD_DIM), v_s.reshape(bs, ts, A_KV_HEADS, A_HEAD_DIM),
                       ki_s, conv_s, gdn_s))

    stack = lambda outs: [jnp.stack([o[i] for o in outs], axis=0) for i in range(5)]
    return (x[:mp].reshape(bp, tp, d), x[mp:].reshape(bs, ts, d), *stack(outs_p), *stack(outs_s))
```

```python
import functools

import jax
import jax.numpy as jnp
from jax import lax
from jax.experimental import pallas as pl
from jax.experimental.pallas import tpu as pltpu

f32 = jnp.float32
bf16 = jnp.bfloat16
i32 = jnp.int32
i16 = jnp.int16

CHUNK = 64
A_HEADS, A_KV_HEADS, A_HEAD_DIM = 8, 2, 128
A_GROUP = A_HEADS // A_KV_HEADS
A_WIDTH = A_HEADS * A_HEAD_DIM
KV_WIDTH = A_KV_HEADS * A_HEAD_DIM
IDX_HEADS, IDX_DIM = 8, 64
TOPK_MAX = 256
B_HEADS, B_KEY_DIM, B_VAL_DIM = 8, 128, 128
B_QK_WIDTH = B_HEADS * B_KEY_DIM
B_WIDTH = B_HEADS * B_VAL_DIM
CONV_WIDTH = 4
CONV_CH = 2 * B_QK_WIDTH + B_WIDTH
N_MOD = 9
FFN_WEIGHT = 0.5
EPS = 1e-6

COL_QKVB, COL_QA, COL_GA, COL_GB, COL_ZB, COL_KV, COL_QI, COL_SM = 0, 3072, 4096, 6144, 8192, 9216, 9728, 10240
SM_WIDTH = 128
SM_WI, SM_AB, SM_BB = 64, 72, 80
PROJ_COLS = COL_SM + SM_WIDTH

INT_MIN = -(2 ** 31)
INT_MAX = 2 ** 31 - 1
I16_MIN, I16_MAX = -(2 ** 15), 2 ** 15 - 1
NEG = -1e30
MASKED_BF16_BITS = -3638
LANES = 128
MXU_DEPTH = 256
LOG2E = 1.4426950408889634
DSA_TK = 1408
SWEEP_BLOCKS = 1
ATT_ROWS = 64
VMEM_LIMIT = 56 * 1024 * 1024


def _pick(n, prefs):
    for p in prefs:
        if n % p == 0:
            return p
    raise ValueError(f"no tile of {prefs} divides {n}")


def _params(*sem):
    return pltpu.CompilerParams(dimension_semantics=sem, vmem_limit_bytes=VMEM_LIMIT)


def _rms(y, g):
    return y * lax.rsqrt(jnp.mean(y * y, axis=-1, keepdims=True) + EPS) * g


def _ada_kernel(c_ref, w_ref, b_ref, o_ref):
    o_ref[...] = jnp.dot(c_ref[...].astype(bf16), w_ref[...].astype(bf16),
                         preferred_element_type=f32) + b_ref[...]


def _ada(c, w, b):
    m, d = c.shape
    n = w.shape[1]
    tn = _pick(n, (1024, 512, 256, 128))
    return pl.pallas_call(
        _ada_kernel,
        grid=(n // tn,),
        in_specs=[pl.BlockSpec((m, d), lambda j: (0, 0)),
                  pl.BlockSpec((d, tn), lambda j: (0, j)),
                  pl.BlockSpec((1, tn), lambda j: (0, j))],
        out_specs=pl.BlockSpec((m, tn), lambda j: (0, j)),
        out_shape=jax.ShapeDtypeStruct((m, n), f32),
        compiler_params=_params("arbitrary"),
        name="ada",
    )(c, w, b)


def _norm_mod_kernel(x_ref, mod_ref, g_ref, o_ref, *, sh, sc, nb):
    g = g_ref[...]
    for j in range(nb):
        rows = slice(j * CHUNK, (j + 1) * CHUNK)
        y = _rms(x_ref[rows, :], g)
        o_ref[rows, :] = (y * (1.0 + mod_ref[j, sc:sc + 1, :]) + mod_ref[j, sh:sh + 1, :]).astype(bf16)


def _norm_mod(x, modg, g, sh, sc):
    m, d = x.shape
    tm = _pick(m, (512, 256, 128, 64))
    nb = tm // CHUNK
    return pl.pallas_call(
        functools.partial(_norm_mod_kernel, sh=sh, sc=sc, nb=nb),
        grid=(m // tm,),
        in_specs=[pl.BlockSpec((tm, d), lambda i: (i, 0)),
                  pl.BlockSpec((nb, N_MOD, d), lambda i: (i, 0, 0)),
                  pl.BlockSpec((1, d), lambda i: (0, 0))],
        out_specs=pl.BlockSpec((tm, d), lambda i: (i, 0)),
        out_shape=jax.ShapeDtypeStruct((m, d), bf16),
        compiler_params=_params("arbitrary"),
        name="norm_mod",
    )(x, modg, g)


def _gateup_kernel(h_ref, wg_ref, wu_ref, o_ref):
    h = h_ref[...]
    a = jnp.dot(h, wg_ref[...], preferred_element_type=f32)
    u = jnp.dot(h, wu_ref[...], preferred_element_type=f32)
    o_ref[...] = (a * jax.nn.sigmoid(a) * u).astype(bf16)


def _gateup(h, wg, wu):
    m, d = h.shape
    n = wg.shape[1]
    tm = _pick(m, (1024, 512, 256, 128, 64))
    tn = _pick(n, (512, 256, 128))
    return pl.pallas_call(
        _gateup_kernel,
        grid=(m // tm, n // tn),
        in_specs=[pl.BlockSpec((tm, d), lambda i, j: (i, 0)),
                  pl.BlockSpec((d, tn), lambda i, j: (0, j)),
                  pl.BlockSpec((d, tn), lambda i, j: (0, j))],
        out_specs=pl.BlockSpec((tm, tn), lambda i, j: (i, j)),
        out_shape=jax.ShapeDtypeStruct((m, n), bf16),
        compiler_params=_params("arbitrary", "arbitrary"),
        name="gateup",
    )(h, wg, wu)


def _mm_norm_res_kernel(a_ref, w_ref, x_ref, mod_ref, g_ref, o_ref, y_scr, *, gt, coeff, nb, nj, tn):
    j = pl.program_id(1)
    y_scr[:, pl.ds(pl.multiple_of(j * tn, tn), tn)] = jnp.dot(a_ref[...], w_ref[...], preferred_element_type=f32)

    @pl.when(j == nj - 1)
    def _():
        g = g_ref[...]
        for jj in range(nb):
            rows = slice(jj * CHUNK, (jj + 1) * CHUNK)
            gate = mod_ref[jj, gt:gt + 1, :]
            if coeff is not None:
                gate = coeff * gate
            o_ref[rows, :] = x_ref[rows, :] + gate * _rms(y_scr[rows, :], g)


def _mm_norm_res(a, w, x, modg, g, gt, coeff):
    m, kd = a.shape
    d = w.shape[1]
    tm = _pick(m, (512, 256, 128, 64))
    tn = _pick(d, (512, 256, 128))
    nb, nj = tm // CHUNK, d // tn
    return pl.pallas_call(
        functools.partial(_mm_norm_res_kernel, gt=gt, coeff=coeff, nb=nb, nj=nj, tn=tn),
        grid=(m // tm, nj),
        in_specs=[pl.BlockSpec((tm, kd), lambda i, j: (i, 0)),
                  pl.BlockSpec((kd, tn), lambda i, j: (0, j)),
                  pl.BlockSpec((tm, d), lambda i, j: (i, 0)),
                  pl.BlockSpec((nb, N_MOD, d), lambda i, j: (i, 0, 0)),
                  pl.BlockSpec((1, d), lambda i, j: (0, 0))],
        out_specs=pl.BlockSpec((tm, d), lambda i, j: (i, 0)),
        out_shape=jax.ShapeDtypeStruct((m, d), f32),
        scratch_shapes=[pltpu.VMEM((tm, d), f32)],
        compiler_params=_params("arbitrary", "arbitrary"),
        name="mm_norm_res",
    )(a, w, x, modg, g)


def _mm_kernel(a_ref, w_ref, o_ref):
    o_ref[...] = jnp.dot(a_ref[...], w_ref[...], preferred_element_type=f32)


def _mm(a, w):
    m, kd = a.shape
    n = w.shape[1]
    tm = _pick(m, (1024, 512, 256, 128, 64))
    tn = _pick(n, (1152, 1024, 512, 384, 256, 128))
    return pl.pallas_call(
        _mm_kernel,
        grid=(m // tm, n // tn),
        in_specs=[pl.BlockSpec((tm, kd), lambda i, j: (i, 0)),
                  pl.BlockSpec((kd, tn), lambda i, j: (0, j))],
        out_specs=pl.BlockSpec((tm, tn), lambda i, j: (i, j)),
        out_shape=jax.ShapeDtypeStruct((m, n), f32),
        compiler_params=_params("arbitrary", "arbitrary"),
        name="in_proj",
    )(a, w)


def _kidx_kernel(sm_ref, g_ref, b_ref, o_ref):
    x = sm_ref[:, 0:IDX_DIM]
    xc = x - jnp.mean(x, axis=-1, keepdims=True)
    y = xc * lax.rsqrt(jnp.mean(xc * xc, axis=-1, keepdims=True) + EPS)
    o_ref[...] = y * g_ref[...] + b_ref[...]


def _kidx(proj, g, b):
    m = proj.shape[0]
    tm = _pick(m, (1024, 512, 256, 128, 64))
    return pl.pallas_call(
        _kidx_kernel,
        grid=(m // tm,),
        in_specs=[pl.BlockSpec((tm, SM_WIDTH), lambda i: (i, COL_SM // SM_WIDTH)),
                  pl.BlockSpec((1, IDX_DIM), lambda i: (0, 0)),
                  pl.BlockSpec((1, IDX_DIM), lambda i: (0, 0))],
        out_specs=pl.BlockSpec((tm, IDX_DIM), lambda i: (i, 0)),
        out_shape=jax.ShapeDtypeStruct((m, IDX_DIM), f32),
        compiler_params=_params("arbitrary"),
        name="kidx_ln",
    )(proj, g, b)


def _dsa_kernel(qi_ref, sm_ref, qa_ref, kit_ref, kt_ref, v_ref, o_ref,
                hi_ref, lo_ref, w_scr, q_scr, p_scr, m_scr, acc_scr, *, tq, tk, past, n_keys, topk, nbits):
    chunk_shift = CHUNK.bit_length() - 1
    nlc = tk // LANES
    qpos0 = past + pl.program_id(1) * tq
    limit = jnp.minimum((((qpos0 + tq - 1) >> chunk_shift) + 1) * CHUNK, n_keys)
    nkb = (limit + tk - 1) // tk
    row = lax.broadcasted_iota(i32, (tq, LANES), 0)
    lane = lax.broadcasted_iota(i32, (tq, LANES), 1)
    qchunk = (qpos0 + row) >> chunk_shift

    def lane_chunks(kb, width=tk):
        c0 = pl.multiple_of(kb * width, width)
        return c0, [pl.multiple_of(c0 + c * LANES, LANES) for c in range(width // LANES)]

    nsb = (nkb + SWEEP_BLOCKS - 1) // SWEEP_BLOCKS
    sweep_chunks = lambda sb: lane_chunks(sb, SWEEP_BLOCKS * tk)[1]

    qi = qi_ref[...].astype(bf16)
    qis = [qi[:, h * IDX_DIM:(h + 1) * IDX_DIM] for h in range(IDX_HEADS)]
    w = sm_ref[:, SM_WI:SM_WI + IDX_HEADS] * (IDX_HEADS ** -0.5 * IDX_DIM ** -0.5)
    for h in range(IDX_HEADS):
        w_scr[h] = jnp.broadcast_to(w[:, h:h + 1], (tq, LANES))

    def score_body(masked, kb, carry):
        c0, ccs = lane_chunks(kb)
        kit = kit_ref[:, pl.ds(c0, tk)]
        sc = [jnp.zeros((tq, LANES), f32)] * nlc
        for h in range(IDX_HEADS):
            s = jnp.dot(qis[h], kit, preferred_element_type=f32)
            wh = w_scr[h]
            sc = [sc[c] + wh * jnp.maximum(s[:, c * LANES:(c + 1) * LANES], 0.0) for c in range(nlc)]
        for c, cc in enumerate(ccs):
            bits = lax.bitcast_convert_type(sc[c], i32)
            key = jnp.where(bits < 0, INT_MIN - bits, bits)
            if masked:
                kpos = cc + lane
                allowed = ((kpos >> chunk_shift) <= qchunk) & (kpos < n_keys)
                key = jnp.where(allowed, key, INT_MIN)
            hi_ref[:, pl.ds(cc, LANES)] = (key >> 16).astype(i16)
            lo_ref[:, pl.ds(cc, LANES)] = (key ^ 0x8000).astype(i16)
        return carry

    nfull = jnp.minimum((qpos0 >> chunk_shift) * CHUNK, n_keys) // tk
    lax.fori_loop(0, nfull, functools.partial(score_body, False), 0)
    lax.fori_loop(nfull, nkb, functools.partial(score_body, True), 0)

    def fill_body(kb, carry):
        for cc in lane_chunks(kb)[1]:
            hi_ref[:, pl.ds(cc, LANES)] = jnp.full((tq, LANES), I16_MIN, i16)
            lo_ref[:, pl.ds(cc, LANES)] = jnp.full((tq, LANES), I16_MIN, i16)
        return carry

    lax.fori_loop(nkb, nsb * SWEEP_BLOCKS, fill_body, 0)

    def count16(pred):
        def body(sb, acc):
            for cc in sweep_chunks(sb):
                acc = acc + jnp.where(pred(cc), jnp.int16(1), jnp.int16(0))
            return acc
        acc = lax.fori_loop(0, nsb, body, jnp.zeros((tq, LANES), i16))
        return jnp.broadcast_to(jnp.sum(acc.astype(f32), axis=1, keepdims=True), (tq, LANES))

    hi_at = lambda cc: hi_ref[:, pl.ds(cc, LANES)]
    lo_at = lambda cc: lo_ref[:, pl.ds(cc, LANES)]
    kpos16 = lambda cc: (cc + lane).astype(i16)
    kf = float(topk)

    def bisect16(at, ct0):
        def bit_body(it, carry):
            t, ct = carry
            cand = t + lax.shift_left(jnp.int32(1), 15 - it)
            c16 = cand.astype(i16)
            cnt = count16(lambda cc: at(cc) >= c16)
            take = cnt >= kf
            return jnp.where(take, cand, t), jnp.where(take, cnt, ct)
        return lax.fori_loop(0, 16, bit_body, (jnp.full((tq, LANES), I16_MIN, i32), ct0))

    th, ct = bisect16(hi_at, jnp.zeros((tq, LANES), f32) + (nsb * (SWEEP_BLOCKS * tk)).astype(f32))
    th16 = th.astype(i16)

    def mask_body(sb, carry):
        for cc in sweep_chunks(sb):
            hi = hi_at(cc)
            lo_ref[:, pl.ds(cc, LANES)] = jnp.where(hi > th16, jnp.int16(I16_MAX),
                                                    jnp.where(hi == th16, lo_at(cc), jnp.int16(I16_MIN)))
        return carry

    lax.fori_loop(0, nsb, mask_body, 0)
    tl, ct = bisect16(lo_at, ct)
    tl16 = tl.astype(i16)

    def tie_cut():
        r = kf - count16(lambda cc: (hi_at(cc) > th16) | ((hi_at(cc) == th16) & (lo_at(cc) > tl16)))
        rr = lax.broadcasted_iota(i32, (LANES, 2 * LANES), 0)
        cl = lax.broadcasted_iota(i32, (LANES, 2 * LANES), 1)
        prefix_and_total = jnp.where((cl >= LANES) | (rr <= cl), 1.0, 0.0).astype(bf16)

        def body(kb, carry):
            seen, jmax = carry
            _, ccs = lane_chunks(kb)
            for cc in ccs:
                eq = (hi_at(cc) == th16) & (lo_at(cc) == tl16)
                e = jnp.where(eq, jnp.int16(1), jnp.int16(0)).astype(i32).astype(f32)
                res = jnp.dot(e.astype(bf16), prefix_and_total, preferred_element_type=f32)
                rank = seen + res[:, 0:LANES]
                pos = (cc + lane).astype(f32)
                jmax = jnp.maximum(jmax, jnp.where((e > 0.0) & (rank <= r), pos, -1.0))
                seen = seen + res[:, LANES:2 * LANES]
            return seen, jmax

        _, jmax = lax.fori_loop(0, nkb, body, (jnp.zeros((tq, LANES), f32), jnp.full((tq, LANES), -1.0, f32)))
        return jnp.broadcast_to(jnp.max(jmax, axis=1, keepdims=True), (tq, LANES)).astype(i32)

    jcut = lax.cond(jnp.max(ct) > kf, tie_cut, lambda: jnp.full((tq, LANES), I16_MAX, i32)).astype(i16)
    tls16 = jnp.where(th == I16_MIN, jnp.maximum(tl, I16_MIN + 1), tl).astype(i16)

    def bias_body(sb, carry):
        for cc in sweep_chunks(sb):
            hi, lo = hi_at(cc), lo_at(cc)
            sel = (hi > th16) | ((hi == th16) & ((lo > tls16) | ((lo == tls16) & (kpos16(cc) <= jcut))))
            hi_ref[:, pl.ds(cc, LANES)] = jnp.where(sel, jnp.int16(0), jnp.int16(MASKED_BF16_BITS))
        return carry

    lax.fori_loop(0, nsb, bias_body, 0)

    qa = qa_ref[...] * (A_HEAD_DIM ** -0.5 * LOG2E)
    for n in range(A_KV_HEADS):
        q_scr[n] = jnp.concatenate(
            [qa[:, (n * A_GROUP + g) * A_HEAD_DIM:(n * A_GROUP + g + 1) * A_HEAD_DIM] for g in range(A_GROUP)],
            axis=0).astype(bf16)
    m_scr[...] = jnp.full(m_scr.shape, NEG, f32)
    acc_scr[...] = jnp.zeros(acc_scr.shape, f32)

    def att_body(kb, carry):
        c0, ccs = lane_chunks(kb)
        bias = [lax.bitcast_convert_type(hi_at(cc), bf16).astype(f32) for cc in ccs]
        heads = range(A_KV_HEADS)
        ar = min(ATT_ROWS, tq)
        blocks = [(slice(blk * ar, (blk + 1) * ar), slice(blk * ar % tq, blk * ar % tq + ar))
                  for blk in range(A_GROUP * tq // ar)]
        s_all = [jnp.dot(q_scr[n], kt_ref[n * A_HEAD_DIM:(n + 1) * A_HEAD_DIM, pl.ds(c0, tk)],
                         preferred_element_type=f32) for n in heads]
        m_news, alphas = [[] for _ in heads], [[] for _ in heads]
        for n in heads:
            for rows, qrows in blocks:
                mx = s_all[n][rows, 0:LANES] + bias[0][qrows]
                for c in range(1, nlc):
                    mx = jnp.maximum(mx, s_all[n][rows, c * LANES:(c + 1) * LANES] + bias[c][qrows])
                m_old = m_scr[n, rows]
                m_new = jnp.maximum(m_old, jnp.max(mx, axis=1, keepdims=True))
                m_scr[n, rows] = m_new
                m_news[n].append(m_new)
                alphas[n].append(jnp.exp2(m_old - m_new))
        for n in heads:
            for blk, (rows, qrows) in enumerate(blocks):
                for c in range(nlc):
                    p = jnp.exp2(s_all[n][rows, c * LANES:(c + 1) * LANES] - (m_news[n][blk] - bias[c][qrows]))
                    p_scr[n, rows, c * LANES:(c + 1) * LANES] = p.astype(bf16)
        for n in heads:
            vv = v_ref[pl.ds(c0, tk), n * 2 * A_HEAD_DIM:(n + 1) * 2 * A_HEAD_DIM]
            alpha = jnp.concatenate(alphas[n], axis=0)
            acc_scr[n] = (jnp.concatenate([alpha, alpha], axis=1) * acc_scr[n]
                          + jnp.dot(p_scr[n], vv, preferred_element_type=f32))
        return carry

    lax.fori_loop(0, nkb, att_body, 0)

    for n in range(A_KV_HEADS):
        o = acc_scr[n, :, 0:A_HEAD_DIM] / acc_scr[n, :, A_HEAD_DIM:2 * A_HEAD_DIM]
        for g in range(A_GROUP):
            hh = n * A_GROUP + g
            o_ref[:, hh * A_HEAD_DIM:(hh + 1) * A_HEAD_DIM] = o[g * tq:(g + 1) * tq].astype(bf16)


def _dsa(proj, row0, nbatch, t_len, kit, kt, v, *, tk, past, n_keys, topk):
    lp = v.shape[1]
    tq = _pick(t_len, (128, 64))
    assert lp % tk == 0 and tk % LANES == 0 and row0 % tq == 0 and past % CHUNK == 0
    assert lp + SWEEP_BLOCKS * tk <= I16_MAX and tq % min(ATT_ROWS, tq) == 0
    nq = t_len // tq
    rb = row0 // tq
    rows = A_GROUP * tq
    wide = -(-lp // (SWEEP_BLOCKS * tk)) * (SWEEP_BLOCKS * tk)
    wide += LANES * (1 - wide // LANES % 2)

    def qmap(width_col):
        return lambda b, i: (rb + b * nq + i, width_col)

    resident = dict(pipeline_mode=pl.Buffered(1)) if nbatch == 1 else {}
    return pl.pallas_call(
        functools.partial(_dsa_kernel, tq=tq, tk=tk, past=past, n_keys=n_keys, topk=topk,
                          nbits=wide.bit_length()),
        grid=(nbatch, nq),
        in_specs=[pl.BlockSpec((tq, IDX_HEADS * IDX_DIM), qmap(COL_QI // (IDX_HEADS * IDX_DIM))),
                  pl.BlockSpec((tq, SM_WIDTH), qmap(COL_SM // SM_WIDTH)),
                  pl.BlockSpec((tq, A_WIDTH), qmap(COL_QA // A_WIDTH)),
                  pl.BlockSpec((None, IDX_DIM, lp), lambda b, i: (b, 0, 0), **resident),
                  pl.BlockSpec((None, KV_WIDTH, lp), lambda b, i: (b, 0, 0), **resident),
                  pl.BlockSpec((None, lp, 2 * KV_WIDTH), lambda b, i: (b, 0, 0), **resident)],
        out_specs=pl.BlockSpec((tq, A_WIDTH), lambda b, i: (b * nq + i, 0)),
        out_shape=jax.ShapeDtypeStruct((nbatch * t_len, A_WIDTH), bf16),
        scratch_shapes=[pltpu.VMEM((tq, wide), i16),
                        pltpu.VMEM((tq, wide), i16),
                        pltpu.VMEM((IDX_HEADS, tq, LANES), f32),
                        pltpu.VMEM((A_KV_HEADS, rows, A_HEAD_DIM), bf16),
                        pltpu.VMEM((A_KV_HEADS, rows, tk), bf16),
                        pltpu.VMEM((A_KV_HEADS, rows, LANES), f32),
                        pltpu.VMEM((A_KV_HEADS, rows, 2 * A_HEAD_DIM), f32)],
        compiler_params=_params("arbitrary", "arbitrary"),
        name="dsa",
    )(proj, proj, proj, kit, kt, v)


_NN = (((1,), (0,)), ((), ()))
_NT = (((1,), (1,)), ((), ()))
_TN = (((0,), (0,)), ((), ()))


def _split(a):
    hi = a.astype(bf16)
    return hi, (a - hi.astype(f32)).astype(bf16)


def _dot1(a, b, dims=_NN):
    return lax.dot_general(a.astype(bf16), b.astype(bf16), dims, preferred_element_type=f32)


def _dot3_sq(a, b):
    ah, al = _split(a)
    bh, bl = _split(b)
    pad = MXU_DEPTH - 3 * CHUNK
    lhs = jnp.concatenate([ah, ah, al, jnp.zeros((CHUNK, pad), bf16)], axis=1)
    rhs = jnp.concatenate([bh, bl, bh, jnp.zeros((pad, CHUNK), bf16)], axis=0)
    return jnp.dot(lhs, rhs, preferred_element_type=f32)


def _gdn_kernel(qkv_ref, sm_ref, z_ref, cw_ref, alog_ref, dtb_ref, gn_ref, buf_ref, s0_ref,
                ob_ref, sout_ref, xp_scr, s_scr, *, nc):
    c = pl.program_id(1)
    chunk_shift = CHUNK.bit_length() - 1

    @pl.when(c == 0)
    def _():
        xp_scr[0:8, :] = buf_ref[...]
        s_scr[...] = s0_ref[...]

    x = qkv_ref[...]
    xp_scr[8:8 + CHUNK, :] = x
    cw = cw_ref[...]
    y = (cw[0:1] * xp_scr[5:5 + CHUNK, :] + cw[1:2] * xp_scr[6:6 + CHUNK, :]
         + cw[2:3] * xp_scr[7:7 + CHUNK, :] + cw[3:4] * x)
    xp_scr[0:8, :] = x[CHUNK - 8:CHUNK, :]
    y = y * jax.nn.sigmoid(y)

    sm = sm_ref[...]
    z = sm + dtb_ref[...]
    softplus = jnp.maximum(z, 0.0) + jnp.log1p(jnp.exp(-jnp.abs(z)))
    g_all = -jnp.exp(alog_ref[...]) * softplus
    beta_all = jax.nn.sigmoid(sm)
    r = lax.broadcasted_iota(i32, (CHUNK, CHUNK), 0)
    cc = lax.broadcasted_iota(i32, (CHUNK, CHUNK), 1)
    tri, strict = r >= cc, r > cc
    eye = jnp.where(r == cc, 1.0, 0.0)
    gc = lax.dot_general(jnp.where(tri, 1.0, 0.0), g_all, _NN,
                         precision=lax.Precision.HIGHEST, preferred_element_type=f32)
    gc_t = gc.T
    g_last = gc[CHUNK - 1:CHUNK, :]
    e_gc, e_last, e_rest = jnp.exp(gc), jnp.exp(g_last), jnp.exp(g_last - gc)
    gn = gn_ref[...]

    heads = range(B_HEADS)
    hs = [slice(h * B_KEY_DIM, (h + 1) * B_KEY_DIM) for h in heads]
    col = lambda a, h: a[:, SM_AB + h:SM_AB + h + 1]
    qs, ks, vbs, kbs, decays = [], [], [], [], []
    for h in heads:
        q = y[:, hs[h]]
        k = y[:, B_QK_WIDTH + h * B_KEY_DIM:B_QK_WIDTH + (h + 1) * B_KEY_DIM]
        v = y[:, 2 * B_QK_WIDTH + h * B_VAL_DIM:2 * B_QK_WIDTH + (h + 1) * B_VAL_DIM]
        beta = beta_all[:, SM_BB + h:SM_BB + h + 1]
        qs.append(q * lax.rsqrt(jnp.sum(q * q, axis=-1, keepdims=True) + EPS) * (B_KEY_DIM ** -0.5))
        k = k * lax.rsqrt(jnp.sum(k * k, axis=-1, keepdims=True) + EPS)
        ks.append(k)
        kbs.append(k * beta)
        vbs.append(v * beta)
        diff = col(gc, h) - gc_t[SM_AB + h:SM_AB + h + 1, :]
        decays.append(jnp.where(tri, jnp.exp(jnp.where(tri, diff, 0.0)), 0.0))
    lows = [jnp.where(strict, _dot1(kbs[h], ks[h], _NT) * decays[h], 0.0) for h in heads]
    xpows, ainvs = lows, [eye - low for low in lows]
    for _ in range(chunk_shift - 1):
        xpows = [_dot3_sq(xp, xp) for xp in xpows]
        ainvs = [ainvs[h] + _dot3_sq(ainvs[h], xpows[h]) for h in heads]
    sols = [_dot1(ainvs[h], jnp.concatenate([vbs[h], kbs[h] * col(e_gc, h)], axis=1)) for h in heads]
    qks = [_dot1(qs[h], ks[h], _NT) * decays[h] for h in heads]
    states = [s_scr[h] for h in heads]
    v_news = [sols[h][:, :B_VAL_DIM] - _dot1(sols[h][:, B_VAL_DIM:], states[h]) for h in heads]
    outs = [_dot1(qs[h] * col(e_gc, h), states[h]) + _dot1(qks[h], v_news[h]) for h in heads]
    for h in heads:
        s_scr[h] = states[h] * col(e_last, h) + _dot1(ks[h] * col(e_rest, h), v_news[h], _TN)
    for h in heads:
        zh = z_ref[:, hs[h]]
        ob_ref[:, hs[h]] = (_rms(outs[h], gn) * (zh * jax.nn.sigmoid(zh))).astype(bf16)

    @pl.when(c == nc - 1)
    def _():
        sout_ref[...] = s_scr[...]


def _gdn(proj, row0, nbatch, t_len, conv_w, alog_row, dtb_row, gn, buf8, s0):
    assert t_len % CHUNK == 0 and row0 % CHUNK == 0
    nc = t_len // CHUNK
    rb = row0 // CHUNK

    def rmap(col):
        return lambda b, c: (rb + b * nc + c, col)

    return pl.pallas_call(
        functools.partial(_gdn_kernel, nc=nc),
        grid=(nbatch, nc),
        in_specs=[pl.BlockSpec((CHUNK, CONV_CH), rmap(COL_QKVB // CONV_CH)),
                  pl.BlockSpec((CHUNK, SM_WIDTH), rmap(COL_SM // SM_WIDTH)),
                  pl.BlockSpec((CHUNK, B_WIDTH), rmap(COL_ZB // B_WIDTH)),
                  pl.BlockSpec((CONV_WIDTH, CONV_CH), lambda b, c: (0, 0)),
                  pl.BlockSpec((1, SM_WIDTH), lambda b, c: (0, 0)),
                  pl.BlockSpec((1, SM_WIDTH), lambda b, c: (0, 0)),
                  pl.BlockSpec((1, B_VAL_DIM), lambda b, c: (0, 0)),
                  pl.BlockSpec((None, 8, CONV_CH), lambda b, c: (b, 0, 0)),
                  pl.BlockSpec((None, B_HEADS, B_KEY_DIM, B_VAL_DIM), lambda b, c: (b, 0, 0, 0))],
        out_specs=[pl.BlockSpec((CHUNK, B_WIDTH), lambda b, c: (b * nc + c, 0)),
                   pl.BlockSpec((None, B_HEADS, B_KEY_DIM, B_VAL_DIM), lambda b, c: (b, 0, 0, 0))],
        out_shape=[jax.ShapeDtypeStruct((nbatch * t_len, B_WIDTH), bf16),
                   jax.ShapeDtypeStruct((nbatch, B_HEADS, B_KEY_DIM, B_VAL_DIM), f32)],
        scratch_shapes=[pltpu.VMEM((8 + CHUNK, CONV_CH), f32),
                        pltpu.VMEM((B_HEADS, B_KEY_DIM, B_VAL_DIM), f32)],
        compiler_params=_params("arbitrary", "arbitrary"),
        name="gdn",
    )(proj, proj, proj, conv_w, alog_row, dtb_row, gn, buf8, s0)


def _merge_kernel(oa_ref, ob_ref, wa_ref, wb_ref, ga_ref, gb_ref, o_ref):
    a = jnp.dot(oa_ref[...], wa_ref[...], preferred_element_type=f32)
    b = jnp.dot(ob_ref[...], wb_ref[...], preferred_element_type=f32)
    o_ref[...] = (jax.nn.sigmoid(ga_ref[...]) * a + jax.nn.sigmoid(gb_ref[...]) * b).astype(bf16)


def _merge(oa, ob, wa, wb, proj):
    m = oa.shape[0]
    d = wa.shape[1]
    tm = _pick(m, (1024, 512, 256, 128, 64))
    tn = 512
    assert d % tn == 0 and COL_GA % tn == 0 and COL_GB % tn == 0
    return pl.pallas_call(
        _merge_kernel,
        grid=(m // tm, d // tn),
        in_specs=[pl.BlockSpec((tm, A_WIDTH), lambda i, j: (i, 0)),
                  pl.BlockSpec((tm, B_WIDTH), lambda i, j: (i, 0)),
                  pl.BlockSpec((A_WIDTH, tn), lambda i, j: (0, j)),
                  pl.BlockSpec((B_WIDTH, tn), lambda i, j: (0, j)),
                  pl.BlockSpec((tm, tn), lambda i, j: (i, COL_GA // tn + j)),
                  pl.BlockSpec((tm, tn), lambda i, j: (i, COL_GB // tn + j))],
        out_specs=pl.BlockSpec((tm, tn), lambda i, j: (i, j)),
        out_shape=jax.ShapeDtypeStruct((m, d), bf16),
        compiler_params=_params("arbitrary", "arbitrary"),
        name="merge",
    )(oa, ob, wa, wb, proj, proj)


def _reorder_w_in(w):
    o, pts = 0, {}
    for name, width in (("qa", A_WIDTH), ("ka", KV_WIDTH), ("va", KV_WIDTH), ("qi", IDX_HEADS * IDX_DIM),
                        ("ki", IDX_DIM), ("wi", IDX_HEADS), ("qb", B_QK_WIDTH), ("kb", B_QK_WIDTH),
                        ("vb", B_WIDTH), ("zb", B_WIDTH), ("ab", B_HEADS), ("bb", B_HEADS)):
        pts[name] = (o, o + width)
        o += width
    d = w.shape[0]
    rest = w.shape[1] - o
    assert rest == 2 * d
    seg = lambda a, b: w[:, pts[a][0]:pts[b][1]]
    used = IDX_DIM + IDX_HEADS + 2 * B_HEADS
    out = jnp.concatenate([seg("qb", "vb"), seg("qa", "qa"), w[:, o:o + d], w[:, o + d:o + 2 * d], seg("zb", "zb"),
                           seg("ka", "va"), seg("qi", "qi"), seg("ki", "wi"), seg("ab", "bb"),
                           jnp.zeros((d, SM_WIDTH - used), w.dtype)], axis=1)
    assert out.shape[1] == PROJ_COLS and COL_GA == COL_QA + A_WIDTH and COL_GB == COL_GA + d
    return out.astype(bf16)


def _lane_row(vals, start):
    return jnp.zeros((1, SM_WIDTH), f32).at[0, start:start + vals.shape[0]].set(vals.astype(f32))


def kernel(x_prompt, x_sample, cache_k, cache_v, cache_kidx, state_conv, state_gdn, c_prompt, c_sample, w_ada, b_ada, norm_g, w1_gate, w1_up, w1_down, w_in, idx_ln_g, idx_ln_b, conv_w, a_log, dt_bias, gdn_norm_g, w_branch_a, w_branch_b, w_out, w2_gate, w2_up, w2_down):
    bp, tp, d = x_prompt.shape
    bs, ts, _ = x_sample.shape
    depth = w_ada.shape[0]
    past = cache_k.shape[2]
    assert tp % CHUNK == 0 and ts % CHUNK == 0 and ts >= CONV_WIDTH - 1 and tp >= CONV_WIDTH - 1
    mp, ms = bp * tp, bs * ts

    x = jnp.concatenate([x_prompt.reshape(mp, d), x_sample.reshape(ms, d)], axis=0)
    c_all = jnp.concatenate([c_prompt, c_sample], axis=0)
    c_pad = jnp.pad(c_all, ((0, -c_all.shape[0] % 8), (0, 0)))

    outs_p, outs_s = [], []
    for l in range(depth):
        mod = _ada(c_pad, w_ada[l], b_ada[l][None])[:bp + bs].reshape(bp + bs, N_MOD, d)
        modg = jnp.concatenate([jnp.repeat(mod[:bp], tp // CHUNK, axis=0),
                                jnp.repeat(mod[bp:], ts // CHUNK, axis=0)], axis=0)
        ng = lambda i: norm_g[l, i][None]

        h = _norm_mod(x, modg, ng(0), 0, 1)
        act = _gateup(h, w1_gate[l].astype(bf16), w1_up[l].astype(bf16))
        x = _mm_norm_res(act, w1_down[l].astype(bf16), x, modg, ng(1), 2, FFN_WEIGHT)

        h = _norm_mod(x, modg, ng(2), 3, 4)
        proj = _mm(h, _reorder_w_in(w_in[l]))
        kidx = _kidx(proj, idx_ln_g[l][None], idx_ln_b[l][None])

        kv_p = proj[:mp, COL_KV:COL_KV + 2 * KV_WIDTH].reshape(bp, tp, 2 * KV_WIDTH)
        k_p, v_p = kv_p[..., :KV_WIDTH], kv_p[..., KV_WIDTH:]
        ki_p = kidx[:mp].reshape(bp, tp, IDX_DIM)
        kv_s = proj[mp:, COL_KV:COL_KV + 2 * KV_WIDTH].reshape(bs, ts, 2 * KV_WIDTH)
        k_s, v_s = kv_s[..., :KV_WIDTH], kv_s[..., KV_WIDTH:]
        ki_s = kidx[mp:].reshape(bs, ts, IDX_DIM)

        def keys(k_all, v_all, ki_all):
            n = k_all.shape[1]
            lp = -(-n // DSA_TK) * DSA_TK
            lp += DSA_TK * (1 - lp // LANES % 2)
            pad = ((0, 0), (0, lp - n), (0, 0))
            kt = jnp.swapaxes(jnp.pad(k_all.astype(bf16), pad), 1, 2)
            kit = jnp.swapaxes(jnp.pad(ki_all.astype(bf16), pad), 1, 2)
            vb = jnp.pad(v_all.astype(bf16), pad)
            ones = jnp.ones(vb.shape[:2] + (A_HEAD_DIM,), bf16)
            v_ext = jnp.concatenate([x for n in range(A_KV_HEADS)
                                     for x in (vb[..., n * A_HEAD_DIM:(n + 1) * A_HEAD_DIM], ones)], axis=-1)
            return kit, kt, v_ext, n

        kit, kt, vv, n_keys = keys(k_p, v_p, ki_p)
        oa_p = _dsa(proj, 0, bp, tp, kit, kt, vv, tk=DSA_TK, past=0, n_keys=n_keys,
                    topk=min(TOPK_MAX, n_keys // 4))
        kit, kt, vv, n_keys = keys(
            jnp.concatenate([cache_k[l].reshape(bs, past, KV_WIDTH), k_s], axis=1),
            jnp.concatenate([cache_v[l].reshape(bs, past, KV_WIDTH), v_s], axis=1),
            jnp.concatenate([cache_kidx[l], ki_s], axis=1))
        oa_s = _dsa(proj, mp, bs, ts, kit, kt, vv, tk=DSA_TK, past=past, n_keys=n_keys,
                    topk=min(TOPK_MAX, n_keys // 4))

        alog_row, dtb_row = _lane_row(a_log[l], SM_AB), _lane_row(dt_bias[l], SM_AB)
        gn = gdn_norm_g[l][None]
        ob_p, gdn_p = _gdn(proj, 0, bp, tp, conv_w[l], alog_row, dtb_row, gn,
                           jnp.zeros((bp, 8, CONV_CH), f32), jnp.zeros((bp, B_HEADS, B_KEY_DIM, B_VAL_DIM), f32))
        buf8 = jnp.pad(state_conv[l], ((0, 0), (8 - (CONV_WIDTH - 1), 0), (0, 0)))
        ob_s, gdn_s = _gdn(proj, mp, bs, ts, conv_w[l], alog_row, dtb_row, gn, buf8, state_gdn[l])

        merged = _merge(jnp.concatenate([oa_p, oa_s], axis=0), jnp.concatenate([ob_p, ob_s], axis=0),
                        w_branch_a[l].astype(bf16), w_branch_b[l].astype(bf16), proj)
        x = _mm_norm_res(merged, w_out[l].astype(bf16), x, modg, ng(3), 5, None)

        h = _norm_mod(x, modg, ng(4), 6, 7)
        act = _gateup(h, w2_gate[l].astype(bf16), w2_up[l].astype(bf16))
        x = _mm_norm_res(act, w2_down[l].astype(bf16), x, modg, ng(5), 8, FFN_WEIGHT)

        tail = CONV_WIDTH - 1
        conv_p = proj[:mp, :CONV_CH].reshape(bp, tp, CONV_CH)[:, tp - tail:]
        conv_s = proj[mp:, :CONV_CH].reshape(bs, ts, CONV_CH)[:, ts - tail:]
        outs_p.append((k_p.reshape(bp, tp, A_KV_HEADS, A_HEAD_DIM), v_p.reshape(bp, tp, A_KV_HEADS, A_HEAD_DIM),
                       ki_p, conv_p, gdn_p))
        outs_s.append((k_s.reshape(bs, ts, A_KV_HEADS, A_HEAD_DIM), v_s.reshape(bs, ts, A_KV_HEADS, A_HEAD_DIM),
                       ki_s, conv_s, gdn_s))

    stack = lambda outs: [jnp.stack([o[i] for o in outs], axis=0) for i in range(5)]
    return (x[:mp].reshape(bp, tp, d), x[mp:].reshape(bs, ts, d), *stack(outs_p), *stack(outs_s))
```
